```python
import jax, jax.numpy as jnp
from jax import lax
import numpy as np

D_MODEL = 1024
BATCH = 2
SEQ = 8192
DEPTH = 4
DEC_BATCH = 2
DEC_SEQ = 16384
PAST_LEN = 128

HEAD_DIM = 64
NA_HEADS = 8
SWA_HEADS = 8
SWA_KV_HEADS = 2
SWA_GROUP = SWA_HEADS // SWA_KV_HEADS
NA_WIDTH = NA_HEADS * HEAD_DIM
SWA_WIDTH = SWA_HEADS * HEAD_DIM
SWA_KV_WIDTH = SWA_KV_HEADS * HEAD_DIM
MIX_WIDTH = NA_WIDTH + SWA_WIDTH
IN_WIDTH = 3 * NA_WIDTH + SWA_WIDTH + 2 * SWA_KV_WIDTH
D_FF = 4 * D_MODEL
GRID_W = 64
NA_ROWS_MAX = 8
NA_COLS = 16
WINDOW = 128
BLOCK = 128
ROPE_THETA = 10000.0
EPS = 1e-5
NEG = -1e30

kernel_name = "hybrid_na_swa_sink_encoder"


def rms_norm(x, g):
    xf = x.astype(jnp.float32)
    y = xf * lax.rsqrt(jnp.mean(xf * xf, axis=-1, keepdims=True) + EPS)
    return (y * g.astype(jnp.float32)).astype(x.dtype)


def rope(x):
    s = x.shape[1]
    half = HEAD_DIM // 2
    inv = ROPE_THETA ** (-jnp.arange(half, dtype=jnp.float32) / half)
    ang = jnp.arange(s, dtype=jnp.float32)[:, None] * inv[None, :]
    cos = jnp.cos(ang)[None, :, None, :]
    sin = jnp.sin(ang)[None, :, None, :]
    xf = x.astype(jnp.float32)
    x1, x2 = xf[..., :half], xf[..., half:]
    return jnp.concatenate([x1 * cos - x2 * sin, x2 * cos + x1 * sin], axis=-1).astype(x.dtype)


def neighbourhood_attention(q, k, v, rpb):
    b, s, h, dh = q.shape
    rows = s // GRID_W
    wr = min(NA_ROWS_MAX, rows)
    wc = NA_COLS
    qg = q.reshape(b, rows, GRID_W, h, dh)
    kg = k.reshape(b, rows, GRID_W, h, dh)
    vg = v.reshape(b, rows, GRID_W, h, dh)
    cols = np.arange(GRID_W)
    col_start = np.clip(cols - wc // 2, 0, GRID_W - wc)
    col_idx = col_start[:, None] + np.arange(wc)[None, :]
    dc_idx = col_idx - cols[:, None] + (NA_COLS - 1)
    scale = dh ** -0.5
    rpb_f = rpb.astype(jnp.float32)

    def one_row(r):
        rs = jnp.clip(r - wr // 2, 0, rows - wr)
        q_r = lax.dynamic_index_in_dim(qg, r, axis=1, keepdims=False)
        k_rows = lax.dynamic_slice_in_dim(kg, rs, wr, axis=1)
        v_rows = lax.dynamic_slice_in_dim(vg, rs, wr, axis=1)
        k_win = k_rows[:, :, col_idx]
        v_win = v_rows[:, :, col_idx]
        dr_idx = rs + jnp.arange(wr) - r + (NA_ROWS_MAX - 1)
        bias = rpb_f[:, dr_idx][:, :, dc_idx]
        sc = jnp.einsum('bchd,brcjhd->bhcrj', q_r, k_win,
                        preferred_element_type=jnp.float32) * scale
        sc = sc + jnp.transpose(bias, (0, 2, 1, 3))[None]
        p = jax.nn.softmax(sc.reshape(b, h, GRID_W, wr * wc), axis=-1)
        p = p.reshape(b, h, GRID_W, wr, wc).astype(v.dtype)
        return jnp.einsum('bhcrj,brcjhd->bchd', p, v_win)

    out = lax.map(one_row, jnp.arange(rows))
    return jnp.transpose(out, (1, 0, 2, 3, 4)).reshape(b, s, h, dh)


def sliding_window_attention(q, k, v, sink):
    b, s, hq, dh = q.shape
    hkv = k.shape[2]
    g = hq // hkv
    nb = s // BLOCK
    scale = dh ** -0.5
    qb = q.reshape(b, nb, BLOCK, hkv, g, dh)

    def band(t):
        pad = jnp.zeros((b, BLOCK, hkv, dh), t.dtype)
        tp = jnp.concatenate([pad, t, pad], axis=1).reshape(b, nb + 2, BLOCK, hkv, dh)
        return jnp.concatenate([tp[:, :-2], tp[:, 1:-1], tp[:, 2:]], axis=2)

    kb, vb = band(k), band(v)
    sc = jnp.einsum('bnqkgd,bnjkd->bnkgqj', qb, kb,
                    preferred_element_type=jnp.float32) * scale
    blk = jnp.arange(nb)
    qpos = blk[:, None] * BLOCK + jnp.arange(BLOCK)[None, :]
    kpos = (blk[:, None] - 1) * BLOCK + jnp.arange(3 * BLOCK)[None, :]
    diff = qpos[:, :, None] - kpos[:, None, :]
    valid = (jnp.abs(diff) <= WINDOW) & (kpos[:, None, :] >= 0) & (kpos[:, None, :] < s)
    sc = jnp.where(valid[None, :, None, None], sc, NEG)
    sink_l = sink.astype(jnp.float32).reshape(hkv, g)[None, None, :, :, None, None]
    m = jnp.maximum(jnp.max(sc, axis=-1, keepdims=True), sink_l)
    p = jnp.exp(sc - m)
    denom = jnp.sum(p, axis=-1, keepdims=True) + jnp.exp(sink_l - m)
    out = jnp.einsum('bnkgqj,bnjkd->bnqkgd', (p / denom).astype(v.dtype), vb)
    return out.reshape(b, s, hq, dh)


def mixer(h, w_in, rpb, sink, w_out):
    b, s, _ = h.shape
    proj = h @ w_in
    o1 = NA_WIDTH
    o2 = 2 * NA_WIDTH
    o3 = 3 * NA_WIDTH
    o4 = o3 + SWA_WIDTH
    o5 = o4 + SWA_KV_WIDTH
    qa, ka, va, qs, ks, vs = jnp.split(proj, [o1, o2, o3, o4, o5], axis=-1)
    qa = qa.reshape(b, s, NA_HEADS, HEAD_DIM)
    ka = ka.reshape(b, s, NA_HEADS, HEAD_DIM)
    va = va.reshape(b, s, NA_HEADS, HEAD_DIM)
    qs = rope(qs.reshape(b, s, SWA_HEADS, HEAD_DIM))
    ks = rope(ks.reshape(b, s, SWA_KV_HEADS, HEAD_DIM))
    vs = vs.reshape(b, s, SWA_KV_HEADS, HEAD_DIM)
    oa = neighbourhood_attention(qa, ka, va, rpb).reshape(b, s, NA_WIDTH)
    ob = sliding_window_attention(qs, ks, vs, sink).reshape(b, s, SWA_WIDTH)
    return jnp.concatenate([oa, ob], axis=-1) @ w_out


def trunk(x, norm_mix, w_in, rpb, sink, w_out, norm_mlp, w_up, w_down, norm_final):
    for l in range(DEPTH):
        x = x + mixer(rms_norm(x, norm_mix[l]), w_in[l], rpb[l], sink[l], w_out[l])
        hdn = rms_norm(x, norm_mlp[l]) @ w_up[l]
        x = x + jnp.square(jax.nn.relu(hdn)) @ w_down[l]
    return rms_norm(x, norm_final)


def setup_inputs(seed: int = 0) -> dict:
    key = jax.random.key(seed)
    ks = jax.random.split(key, 12)
    f32 = jnp.float32
    x_prompt = jax.random.normal(ks[0], (BATCH, SEQ, D_MODEL), f32)
    x_sample = jax.random.normal(ks[1], (DEC_BATCH, DEC_SEQ, D_MODEL), f32)
    norm_mix = 1.0 + 0.05 * jax.random.normal(ks[2], (DEPTH, D_MODEL), f32)
    w_in = jax.random.normal(ks[3], (DEPTH, D_MODEL, IN_WIDTH), f32) * D_MODEL ** -0.5
    rpb = 0.1 * jax.random.normal(ks[4], (DEPTH, NA_HEADS, 2 * NA_ROWS_MAX - 1, 2 * NA_COLS - 1), f32)
    sink = 0.5 * jax.random.normal(ks[5], (DEPTH, SWA_HEADS), f32)
    w_out = jax.random.normal(ks[6], (DEPTH, MIX_WIDTH, D_MODEL), f32) * MIX_WIDTH ** -0.5
    norm_mlp = 1.0 + 0.05 * jax.random.normal(ks[7], (DEPTH, D_MODEL), f32)
    w_up = jax.random.normal(ks[8], (DEPTH, D_MODEL, D_FF), f32) * D_MODEL ** -0.5
    w_down = jax.random.normal(ks[9], (DEPTH, D_FF, D_MODEL), f32) * D_FF ** -0.5
    norm_final = 1.0 + 0.05 * jax.random.normal(ks[10], (D_MODEL,), f32)
    return {"x_prompt": x_prompt, "x_sample": x_sample, "norm_mix": norm_mix, "w_in": w_in,
            "rpb": rpb, "sink": sink, "w_out": w_out, "norm_mlp": norm_mlp, "w_up": w_up,
            "w_down": w_down, "norm_final": norm_final}


def reference(x_prompt, x_sample, norm_mix, w_in, rpb, sink, w_out, norm_mlp, w_up, w_down, norm_final):
    y_prompt = trunk(x_prompt, norm_mix, w_in, rpb, sink, w_out, norm_mlp, w_up, w_down, norm_final)
    y_sample = trunk(x_sample, norm_mix, w_in, rpb, sink, w_out, norm_mlp, w_up, w_down, norm_final)
    return (y_prompt, y_sample)
```

```python
import functools

import numpy as np
import jax
import jax.numpy as jnp
from jax import lax
from jax.experimental import pallas as pl
from jax.experimental.pallas import tpu as pltpu

D_MODEL = 1024
HEAD_DIM = 64
NA_HEADS = 8
SWA_HEADS = 8
SWA_KV_HEADS = 2
NA_WIDTH = NA_HEADS * HEAD_DIM
SWA_WIDTH = SWA_HEADS * HEAD_DIM
SWA_KV_WIDTH = SWA_KV_HEADS * HEAD_DIM
D_FF = 4 * D_MODEL
GRID_W = 64
NA_ROWS = 8
NA_COLS = 16
WINDOW = 128
ROPE_THETA = 10000.0
EPS = 1e-5
NEG = -1e30

LANES = 128
TOK_TILE = 512
PAIR = 2 * GRID_W
PAIRS_PER_TILE = TOK_TILE // PAIR
NA_WIN_ROWS = 10
NA_WIN_PAIRS = NA_WIN_ROWS // 2
NA_HALO_PAIRS = 2 * PAIRS_PER_TILE
N_BIAS_TILES = 17
VMEM_LIMIT = 56 * 1024 * 1024

TOK_COLS = NA_WIDTH + NA_WIDTH + SWA_WIDTH + SWA_KV_WIDTH
FM_ROWS = NA_WIDTH + SWA_KV_WIDTH


def _rms(x, g):
    return x * lax.rsqrt(jnp.mean(x * x, axis=-1, keepdims=True) + EPS) * g


def _dot_nt(a, b):
    return lax.dot_general(a, b, (((1,), (1,)), ((), ())), preferred_element_type=jnp.float32)


def _dot(a, b):
    return jnp.dot(a, b, preferred_element_type=jnp.float32)


def _rope(x, cos, sin_signed, first_half):
    rot = jnp.where(first_half, pltpu.roll(x, LANES - HEAD_DIM // 2, 1), pltpu.roll(x, HEAD_DIM // 2, 1))
    return x * cos + rot * sin_signed


def _inproj_kernel(pos_ref, x_ref, g_ref, wtok_ref, wfm_ref, cos_ref, sin_ref,
                   qa_ref, ka_ref, qs_ref, ksa_ref, ksb_ref, vat_ref, vst_ref):
    del pos_ref
    h = _rms(x_ref[...], g_ref[...]).astype(jnp.bfloat16)
    proj = _dot(h, wtok_ref[...])
    qa_ref[...] = proj[:, :NA_WIDTH].astype(jnp.bfloat16)
    ka_ref[...] = proj[:, NA_WIDTH:2 * NA_WIDTH].astype(jnp.bfloat16)
    cos = cos_ref[...]
    sin = sin_ref[...]
    lane = lax.broadcasted_iota(jnp.int32, cos.shape, 1)
    first_half = (lane % HEAD_DIM) < (HEAD_DIM // 2)
    base = 2 * NA_WIDTH
    for j in range(SWA_WIDTH // LANES):
        q = proj[:, base + j * LANES: base + (j + 1) * LANES]
        qs_ref[:, j * LANES:(j + 1) * LANES] = _rope(q, cos, sin, first_half).astype(jnp.bfloat16)
    k = _rope(proj[:, base + SWA_WIDTH:], cos, sin, first_half)
    ksa_ref[...] = k.astype(jnp.bfloat16)
    ksb_ref[...] = pltpu.roll(k, HEAD_DIM, 1).astype(jnp.bfloat16)
    fm = _dot_nt(wfm_ref[...], h)
    for j in range(TOK_TILE // PAIR):
        vat_ref[j] = fm[:NA_WIDTH, j * PAIR:(j + 1) * PAIR].astype(jnp.bfloat16)
        vst_ref[j] = fm[NA_WIDTH:, j * PAIR:(j + 1) * PAIR].astype(jnp.bfloat16)


def _inproj(x, g, wtok, wfm, cos, sin, pos_tile):
    t = x.shape[0]
    n = t // TOK_TILE
    npb = t // PAIR
    bf = jnp.bfloat16
    tile = lambda w: pl.BlockSpec((TOK_TILE, w), lambda i, p: (i, 0))
    whole = lambda a: pl.BlockSpec(a.shape, lambda i, p: (0,) * a.ndim)
    rope_spec = pl.BlockSpec((TOK_TILE, LANES), lambda i, p: (p[i], 0))
    fm_spec = lambda r: pl.BlockSpec((PAIRS_PER_TILE, r, PAIR), lambda i, p: (i, 0, 0))
    return pl.pallas_call(
        _inproj_kernel,
        grid_spec=pltpu.PrefetchScalarGridSpec(
            num_scalar_prefetch=1, grid=(n,),
            in_specs=[tile(D_MODEL), whole(g), whole(wtok), whole(wfm), rope_spec, rope_spec],
            out_specs=[tile(NA_WIDTH), tile(NA_WIDTH), tile(SWA_WIDTH), tile(SWA_KV_WIDTH),
                       tile(SWA_KV_WIDTH), fm_spec(NA_WIDTH), fm_spec(SWA_KV_WIDTH)]),
        out_shape=[jax.ShapeDtypeStruct((t, NA_WIDTH), bf), jax.ShapeDtypeStruct((t, NA_WIDTH), bf),
                   jax.ShapeDtypeStruct((t, SWA_WIDTH), bf), jax.ShapeDtypeStruct((t, SWA_KV_WIDTH), bf),
                   jax.ShapeDtypeStruct((t, SWA_KV_WIDTH), bf),
                   jax.ShapeDtypeStruct((npb, NA_WIDTH, PAIR), bf),
                   jax.ShapeDtypeStruct((npb, SWA_KV_WIDTH, PAIR), bf)],
        compiler_params=pltpu.CompilerParams(dimension_semantics=("arbitrary",),
                                             vmem_limit_bytes=VMEM_LIMIT),
        name="inproj",
    )(pos_tile, x, g, wtok, wfm, cos, sin)


def _na_kernel(win_ref, bidx_ref, q_ref, k0_ref, k1_ref, k2_ref, k3_ref,
               v0_ref, v1_ref, v2_ref, v3_ref, bias_ref, o_ref, kbuf, vbuf):
    step = pl.program_id(0)
    half = NA_HALO_PAIRS * PAIR // 4
    for j, (kr, vr) in enumerate(((k0_ref, v0_ref), (k1_ref, v1_ref), (k2_ref, v2_ref), (k3_ref, v3_ref))):
        kbuf[j * half:(j + 1) * half, :] = kr[...]
        vbuf[2 * j:2 * j + 2] = vr[...]
    lane = lax.broadcasted_iota(jnp.int32, (PAIR, LANES), 1)
    low = lane < HEAD_DIM
    zero = jnp.zeros((PAIR, LANES), jnp.bfloat16)

    def pair_body(pp, carry):
        p = step * PAIRS_PER_TILE + pp
        rel = win_ref[p]
        qrow = pl.multiple_of(pp * PAIR, PAIR)
        krow = pl.multiple_of(rel * PAIR, PAIR)
        for hp in range(NA_HEADS // 2):
            cols = slice(hp * LANES, (hp + 1) * LANES)
            qp = q_ref[pl.ds(qrow, PAIR), cols]
            qbd = jnp.concatenate([jnp.where(low, qp, zero), jnp.where(low, zero, qp)], axis=0)
            kp = kbuf[pl.ds(krow, NA_WIN_ROWS * GRID_W), cols]
            s = _dot_nt(kp, qbd)
            pieces = []
            for i in range(NA_WIN_ROWS):
                bi = bidx_ref[p * NA_WIN_ROWS + i]
                b = jnp.concatenate([bias_ref[(2 * hp) * N_BIAS_TILES + bi],
                                     bias_ref[(2 * hp + 1) * N_BIAS_TILES + bi]], axis=1)
                pieces.append(s[i * GRID_W:(i + 1) * GRID_W, :] + b)
            s = jnp.concatenate(pieces, axis=0)
            m = jnp.max(s, axis=0, keepdims=True)
            e = jnp.exp(s - m)
            denom = jnp.sum(e, axis=0, keepdims=True)
            vt = jnp.concatenate([vbuf[rel + j, cols, :] for j in range(NA_WIN_PAIRS)], axis=1)
            ot = _dot(vt, e.astype(jnp.bfloat16)) / denom
            o_pair = jnp.concatenate([ot[:HEAD_DIM, :LANES], ot[HEAD_DIM:, LANES:]], axis=0)
            o_ref[pl.ds(qrow, PAIR), cols] = o_pair.T.astype(jnp.bfloat16)
        return carry

    lax.fori_loop(0, PAIRS_PER_TILE, pair_body, 0)


def _na(qa, ka, vat, bias, win, bidx):
    t = qa.shape[0]
    n = t // TOK_TILE
    hb = NA_HALO_PAIRS * PAIR // 4
    nhb = t // hb
    halo_k = lambda j: pl.BlockSpec((hb, NA_WIDTH), lambda i, w, b: (jnp.clip(2 * i - 1 + j, 0, nhb - 1), 0))
    halo_v = lambda j: pl.BlockSpec((2, NA_WIDTH, PAIR), lambda i, w, b: (jnp.clip(2 * i - 1 + j, 0, nhb - 1), 0, 0))
    tile = pl.BlockSpec((TOK_TILE, NA_WIDTH), lambda i, w, b: (i, 0))
    return pl.pallas_call(
        _na_kernel,
        grid_spec=pltpu.PrefetchScalarGridSpec(
            num_scalar_prefetch=2, grid=(n,),
            in_specs=[tile] + [halo_k(j) for j in range(4)] + [halo_v(j) for j in range(4)]
                     + [pl.BlockSpec(bias.shape, lambda i, w, b: (0, 0, 0))],
            out_specs=tile,
            scratch_shapes=[pltpu.VMEM((NA_HALO_PAIRS * PAIR, NA_WIDTH), jnp.bfloat16),
                            pltpu.VMEM((NA_HALO_PAIRS, NA_WIDTH, PAIR), jnp.bfloat16)]),
        out_shape=jax.ShapeDtypeStruct((t, NA_WIDTH), jnp.bfloat16),
        compiler_params=pltpu.CompilerParams(dimension_semantics=("arbitrary",),
                                             vmem_limit_bytes=VMEM_LIMIT),
        name="na_attn",
    )(win, bidx, qa, ka, ka, ka, ka, vat, vat, vat, vat, bias)


def _swa_kernel(flag_ref, q_ref, kap_ref, kac_ref, kan_ref, kbp_ref, kbc_ref, kbn_ref,
                vp_ref, vc_ref, vn_ref, sink_ref, o_ref, kabuf, kbbuf, vbuf):
    step = pl.program_id(0)
    nblk = TOK_TILE // PAIR
    kabuf[:PAIR] = kap_ref[...]
    kabuf[PAIR:PAIR + TOK_TILE] = kac_ref[...]
    kabuf[PAIR + TOK_TILE:] = kan_ref[...]
    kbbuf[:PAIR] = kbp_ref[...]
    kbbuf[PAIR:PAIR + TOK_TILE] = kbc_ref[...]
    kbbuf[PAIR + TOK_TILE:] = kbn_ref[...]
    vbuf[0:1] = vp_ref[...]
    vbuf[1:1 + nblk] = vc_ref[...]
    vbuf[1 + nblk:] = vn_ref[...]

    nkeys = 3 * PAIR
    ki = lax.broadcasted_iota(jnp.int32, (nkeys, LANES), 0)
    qi = lax.broadcasted_iota(jnp.int32, (nkeys, LANES), 1)
    rel = ki - qi
    band = jnp.where(rel >= 0, jnp.where(rel <= 2 * WINDOW, 0.0, NEG), NEG).astype(jnp.float32)
    in_prev = ki < PAIR
    in_next = ki >= 2 * PAIR
    low = qi < HEAD_DIM
    zero = jnp.zeros((nkeys, LANES), jnp.bfloat16)

    def blk_body(qb, carry):
        flags = flag_ref[step * nblk + qb]
        pen_prev = (flags & 1).astype(jnp.float32) * NEG
        pen_next = ((flags >> 1) & 1).astype(jnp.float32) * NEG
        mask = band + jnp.where(in_prev, pen_prev, 0.0) + jnp.where(in_next, pen_next, 0.0)
        mask2 = jnp.concatenate([mask, mask], axis=1)
        qrow = pl.multiple_of(qb * PAIR, PAIR)
        ka = kabuf[pl.ds(qrow, nkeys), :]
        kb = kbbuf[pl.ds(qrow, nkeys), :]
        vt = jnp.concatenate([vbuf[qb + j] for j in range(3)], axis=1)
        for kv in range(SWA_KV_HEADS):
            q_even = q_ref[pl.ds(qrow, PAIR), (2 * kv) * LANES:(2 * kv + 1) * LANES]
            q_odd = q_ref[pl.ds(qrow, PAIR), (2 * kv + 1) * LANES:(2 * kv + 2) * LANES]
            qcat = jnp.concatenate([q_even, q_odd], axis=0)
            k_lo = jnp.where(low, ka if kv == 0 else kb, zero)
            k_hi = jnp.where(low, zero, kb if kv == 0 else ka)
            probs, denoms = [], []
            for vi, kvar in enumerate((k_lo, k_hi)):
                s = _dot_nt(kvar, qcat) + mask2
                snk = sink_ref[2 * kv + vi:2 * kv + vi + 1, :]
                m = jnp.maximum(jnp.max(s, axis=0, keepdims=True), snk)
                e = jnp.exp(s - m)
                denoms.append(jnp.sum(e, axis=0, keepdims=True) + jnp.exp(snk - m))
                probs.append(e.astype(jnp.bfloat16))
            pt = jnp.concatenate(probs, axis=1)
            ot = _dot(vt, pt)[kv * HEAD_DIM:(kv + 1) * HEAD_DIM, :]
            ot = ot / jnp.concatenate(denoms, axis=1)
            for pair in range(2):
                o_pair = jnp.concatenate([ot[:, pair * LANES:(pair + 1) * LANES],
                                          ot[:, (2 + pair) * LANES:(3 + pair) * LANES]], axis=0)
                o_ref[pl.ds(qrow, PAIR), (2 * kv + pair) * LANES:(2 * kv + pair + 1) * LANES] = (
                    o_pair.T.astype(jnp.bfloat16))
        return carry

    lax.fori_loop(0, nblk, blk_body, 0)


def _swa(qs, ksa, ksb, vst, sinkvec, flags):
    t = qs.shape[0]
    n = t // TOK_TILE
    nblk = TOK_TILE // PAIR
    npb = t // PAIR
    prev = lambda i, f: (jnp.maximum(nblk * i - 1, 0), 0)
    nxt = lambda i, f: (jnp.minimum(nblk * i + nblk, npb - 1), 0)
    cur = lambda i, f: (i, 0)
    kspecs = [pl.BlockSpec((PAIR, SWA_KV_WIDTH), prev), pl.BlockSpec((TOK_TILE, SWA_KV_WIDTH), cur),
              pl.BlockSpec((PAIR, SWA_KV_WIDTH), nxt)]
    vspecs = [pl.BlockSpec((1, SWA_KV_WIDTH, PAIR), lambda i, f: prev(i, f) + (0,)),
              pl.BlockSpec((nblk, SWA_KV_WIDTH, PAIR), lambda i, f: (i, 0, 0)),
              pl.BlockSpec((1, SWA_KV_WIDTH, PAIR), lambda i, f: nxt(i, f) + (0,))]
    tile = pl.BlockSpec((TOK_TILE, SWA_WIDTH), cur)
    return pl.pallas_call(
        _swa_kernel,
        grid_spec=pltpu.PrefetchScalarGridSpec(
            num_scalar_prefetch=1, grid=(n,),
            in_specs=[tile] + kspecs + kspecs + vspecs + [pl.BlockSpec(sinkvec.shape, lambda i, f: (0, 0))],
            out_specs=tile,
            scratch_shapes=[pltpu.VMEM((TOK_TILE + 2 * PAIR, SWA_KV_WIDTH), jnp.bfloat16),
                            pltpu.VMEM((TOK_TILE + 2 * PAIR, SWA_KV_WIDTH), jnp.bfloat16),
                            pltpu.VMEM((nblk + 2, SWA_KV_WIDTH, PAIR), jnp.bfloat16)]),
        out_shape=jax.ShapeDtypeStruct((t, SWA_WIDTH), jnp.bfloat16),
        compiler_params=pltpu.CompilerParams(dimension_semantics=("arbitrary",),
                                             vmem_limit_bytes=VMEM_LIMIT),
        name="swa_attn",
    )(flags, qs, ksa, ksa, ksa, ksb, ksb, ksb, vst, vst, vst, sinkvec)


def _mlp_kernel(x_ref, oa_ref, ob_ref, wout_ref, g_ref, wup_ref, wdn_ref, gf_ref, o_ref, *, final):
    mix = jnp.concatenate([oa_ref[...], ob_ref[...]], axis=1)
    x1 = x_ref[...] + _dot(mix, wout_ref[...])
    h = _rms(x1, g_ref[...]).astype(jnp.bfloat16)
    acc = x1
    for c in range(D_FF // D_MODEL):
        cols = slice(c * D_MODEL, (c + 1) * D_MODEL)
        u = jnp.maximum(_dot(h, wup_ref[:, cols]), 0.0)
        acc = acc + _dot((u * u).astype(jnp.bfloat16), wdn_ref[cols, :])
    if final:
        acc = _rms(acc, gf_ref[...])
    o_ref[...] = acc


def _mlp(x, oa, ob, wout, g, wup, wdn, gf, final):
    t = x.shape[0]
    n = t // TOK_TILE
    tile = lambda w: pl.BlockSpec((TOK_TILE, w), lambda i: (i, 0))
    whole = lambda a: pl.BlockSpec(a.shape, lambda i: (0,) * a.ndim, pipeline_mode=pl.Buffered(1))
    return pl.pallas_call(
        functools.partial(_mlp_kernel, final=final),
        grid=(n,),
        in_specs=[tile(D_MODEL), tile(NA_WIDTH), tile(SWA_WIDTH), whole(wout), whole(g), whole(wup),
                  whole(wdn), whole(gf)],
        out_specs=tile(D_MODEL),
        out_shape=jax.ShapeDtypeStruct((t, D_MODEL), jnp.float32),
        compiler_params=pltpu.CompilerParams(dimension_semantics=("arbitrary",),
                                             vmem_limit_bytes=VMEM_LIMIT),
        name="mlp",
    )(x, oa, ob, wout, g, wup, wdn, gf)


def _static_tables(seq_lens):
    starts = np.concatenate([[0], np.cumsum(seq_lens)[:-1]]).astype(np.int64)
    total = int(np.sum(seq_lens))
    pos_tile = np.zeros(total // TOK_TILE, np.int32)
    win = np.zeros(total // PAIR, np.int32)
    bidx = np.zeros((total // PAIR, NA_WIN_ROWS), np.int32)
    flags = np.zeros(total // PAIR, np.int32)
    for s0, s in zip(starts, seq_lens):
        assert s % TOK_TILE == 0 and s0 % TOK_TILE == 0 and s // GRID_W >= NA_WIN_ROWS
        rows = s // GRID_W
        for i in range(s // TOK_TILE):
            pos_tile[s0 // TOK_TILE + i] = i
        flags[s0 // PAIR] |= 1
        flags[(s0 + s) // PAIR - 1] |= 2
        for q in range(rows // 2):
            p = s0 // PAIR + q
            r0 = 2 * q
            kb = min(max(r0 - NA_ROWS // 2, 0), rows - NA_WIN_ROWS)
            assert kb % 2 == 0
            rel = s0 // PAIR + kb // 2 - ((p // PAIRS_PER_TILE) * PAIRS_PER_TILE - 2)
            assert 0 <= rel and rel + NA_WIN_PAIRS <= NA_HALO_PAIRS
            win[p] = rel
            for i in range(NA_WIN_ROWS):
                kr = kb + i
                ok = []
                for r in (r0, r0 + 1):
                    rs = min(max(r - NA_ROWS // 2, 0), rows - NA_ROWS)
                    ok.append(rs <= kr < rs + NA_ROWS)
                d1 = kr - (r0 + 1) + NA_ROWS - 1
                if ok[0] and ok[1]:
                    assert 0 <= d1 <= 13
                    bidx[p, i] = d1
                elif ok[0]:
                    assert d1 == 2
                    bidx[p, i] = 14
                elif ok[1]:
                    assert d1 == 10
                    bidx[p, i] = 15
                else:
                    bidx[p, i] = 16
    return pos_tile, win, bidx.reshape(-1), flags


def _na_bias_tiles(rpb):
    cols = np.arange(GRID_W)
    cstart = np.clip(cols - NA_COLS // 2, 0, GRID_W - NA_COLS)
    kc = cols[:, None]
    c = cols[None, :]
    col_ok = (kc >= cstart[None, :]) & (kc < cstart[None, :] + NA_COLS)
    dc = np.clip(kc - c + NA_COLS - 1, 0, 2 * NA_COLS - 2)
    tc = jnp.where(col_ok, rpb.astype(jnp.float32)[..., dc], NEG)
    neg = jnp.full(tc.shape[:2] + (1, GRID_W, GRID_W), NEG, jnp.float32)
    both = jnp.concatenate([tc[:, :, 1:15], tc[:, :, 0:14]], axis=-1)
    first_only = jnp.concatenate([tc[:, :, 3:4], neg], axis=-1)
    second_only = jnp.concatenate([neg, tc[:, :, 10:11]], axis=-1)
    none = jnp.concatenate([neg, neg], axis=-1)
    tiles = jnp.concatenate([both, first_only, second_only, none], axis=2)
    l = tiles.shape[0]
    return tiles.reshape(l, NA_HEADS * N_BIAS_TILES, GRID_W, PAIR)


def _rope_tables(max_len):
    half = HEAD_DIM // 2
    inv = ROPE_THETA ** (-jnp.arange(half, dtype=jnp.float32) / half)
    ang = jnp.arange(max_len, dtype=jnp.float32)[:, None] * inv[None, :]
    cos = jnp.tile(jnp.cos(ang), (1, LANES // half))
    sin = jnp.tile(jnp.sin(ang), (1, LANES // half))
    sign = np.where((np.arange(LANES) % HEAD_DIM) < half, -1.0, 1.0).astype(np.float32)
    return cos, sin * sign[None, :]


def kernel(x_prompt, x_sample, norm_mix, w_in, rpb, sink, w_out, norm_mlp, w_up, w_down, norm_final):
    depth = w_in.shape[0]
    seq_lens = [x_prompt.shape[1]] * x_prompt.shape[0] + [x_sample.shape[1]] * x_sample.shape[0]
    pos_tile, win, bidx, flags = (jnp.asarray(a) for a in _static_tables(seq_lens))
    cos, sin = _rope_tables(max(seq_lens))

    bf = jnp.bfloat16
    scale = HEAD_DIM ** -0.5
    o1, o2, o3 = NA_WIDTH, 2 * NA_WIDTH, 3 * NA_WIDTH
    o4 = o3 + SWA_WIDTH
    o5 = o4 + SWA_KV_WIDTH
    wtok = jnp.concatenate([w_in[..., :o1] * scale, w_in[..., o1:o2], w_in[..., o3:o4] * scale,
                            w_in[..., o4:o5]], axis=-1).astype(bf)
    wfm = jnp.swapaxes(jnp.concatenate([w_in[..., o2:o3], w_in[..., o5:]], axis=-1), 1, 2).astype(bf)
    wout = w_out.astype(bf)
    wup = w_up.astype(bf)
    wdn = w_down.astype(bf)
    bias = _na_bias_tiles(rpb)
    heads = np.array([[4 * kv + vi, 4 * kv + 2 + vi] for kv in range(SWA_KV_HEADS) for vi in range(2)])
    sinkvec = jnp.repeat(sink.astype(jnp.float32)[:, heads], PAIR, axis=-1).reshape(depth, 4, 2 * PAIR)
    sinkvec = jnp.concatenate([sinkvec, jnp.zeros_like(sinkvec)], axis=1)

    x = jnp.concatenate([x_prompt.reshape(-1, D_MODEL), x_sample.reshape(-1, D_MODEL)], axis=0)
    gf = norm_final.reshape(1, D_MODEL)
    for l in range(depth):
        qa, ka, qs, ksa, ksb, vat, vst = _inproj(x, norm_mix[l].reshape(1, D_MODEL), wtok[l], wfm[l],
                                                 cos, sin, pos_tile)
        oa = _na(qa, ka, vat, bias[l], win, bidx)
        ob = _swa(qs, ksa, ksb, vst, sinkvec[l], flags)
        x = _mlp(x, oa, ob, wout[l], norm_mlp[l].reshape(1, D_MODEL), wup[l], wdn[l], gf,
                 final=(l == depth - 1))
    tp = x_prompt.shape[0] * x_prompt.shape[1]
    return (x[:tp].reshape(x_prompt.shape), x[tp:].reshape(x_sample.shape))
```

```python
import functools

import numpy as np
import jax
import jax.numpy as jnp
from jax import lax
from jax.experimental import pallas as pl
from jax.experimental.pallas import tpu as pltpu

D_MODEL = 1024
HEAD_DIM = 64
NA_HEADS = 8
SWA_HEADS = 8
SWA_KV_HEADS = 2
NA_WIDTH = NA_HEADS * HEAD_DIM
SWA_WIDTH = SWA_HEADS * HEAD_DIM
SWA_KV_WIDTH = SWA_KV_HEADS * HEAD_DIM
D_FF = 4 * D_MODEL
GRID_W = 64
NA_ROWS = 8
NA_COLS = 16
WINDOW = 128
ROPE_THETA = 10000.0
EPS = 1e-5
NEG = -1e30
LOG2E = 1.4426950408889634

LANES = 128
TOK_TILE = 512
PAIR = 2 * GRID_W
PAIRS_PER_TILE = TOK_TILE // PAIR
NA_WIN_ROWS = 10
NA_WIN_PAIRS = NA_WIN_ROWS // 2
NA_HALO_PAIRS = 2 * PAIRS_PER_TILE
N_BIAS_TILES = 17
VMEM_LIMIT = 56 * 1024 * 1024

TOK_COLS = NA_WIDTH + NA_WIDTH + SWA_WIDTH + SWA_KV_WIDTH
FM_ROWS = NA_WIDTH + SWA_KV_WIDTH


def _rms(x, g):
    return x * lax.rsqrt(jnp.mean(x * x, axis=-1, keepdims=True) + EPS) * g


def _dot_nt(a, b):
    return lax.dot_general(a, b, (((1,), (1,)), ((), ())), preferred_element_type=jnp.float32)


def _dot(a, b):
    return jnp.dot(a, b, preferred_element_type=jnp.float32)


def _token_specs(n_first, width, n_sources):
    if n_sources == 1:
        return [pl.BlockSpec((TOK_TILE, width), lambda i, *_: (i, 0))]
    return [pl.BlockSpec((TOK_TILE, width), lambda i, *_: (jnp.minimum(i, n_first - 1), 0)),
            pl.BlockSpec((TOK_TILE, width), lambda i, *_: (jnp.maximum(i - n_first, 0), 0))]


def _load_tokens(refs, n_first):
    if len(refs) == 1:
        return refs[0][...]
    return jnp.where(pl.program_id(0) < n_first, refs[0][...], refs[1][...])


def _rope(x, cos, sin_signed, first_half):
    rot = jnp.where(first_half, pltpu.roll(x, LANES - HEAD_DIM // 2, 1), pltpu.roll(x, HEAD_DIM // 2, 1))
    return x * cos + rot * sin_signed


def _inproj_kernel(pos_ref, *refs, n_first, n_sources):
    del pos_ref
    x_refs, refs = refs[:n_sources], refs[n_sources:]
    (g_ref, wtok_ref, wfm_ref, cos_ref, sin_ref,
     qa_ref, ka_ref, qs_ref, ksa_ref, ksb_ref, vat_ref, vst_ref) = refs
    h = _rms(_load_tokens(x_refs, n_first), g_ref[...]).astype(jnp.bfloat16)
    proj = _dot(h, wtok_ref[...])
    qa_ref[...] = proj[:, :NA_WIDTH].astype(jnp.bfloat16)
    ka_ref[...] = proj[:, NA_WIDTH:2 * NA_WIDTH].astype(jnp.bfloat16)
    cos = cos_ref[...]
    sin = sin_ref[...]
    lane = lax.broadcasted_iota(jnp.int32, cos.shape, 1)
    first_half = (lane % HEAD_DIM) < (HEAD_DIM // 2)
    base = 2 * NA_WIDTH
    for j in range(SWA_WIDTH // LANES):
        q = proj[:, base + j * LANES: base + (j + 1) * LANES]
        qs_ref[:, j * LANES:(j + 1) * LANES] = _rope(q, cos, sin, first_half).astype(jnp.bfloat16)
    k = _rope(proj[:, base + SWA_WIDTH:], cos, sin, first_half)
    ksa_ref[...] = k.astype(jnp.bfloat16)
    ksb_ref[...] = pltpu.roll(k, HEAD_DIM, 1).astype(jnp.bfloat16)
    fm = _dot_nt(wfm_ref[...], h)
    for j in range(TOK_TILE // PAIR):
        vat_ref[j] = fm[:NA_WIDTH, j * PAIR:(j + 1) * PAIR].astype(jnp.bfloat16)
        vst_ref[j] = fm[NA_WIDTH:, j * PAIR:(j + 1) * PAIR].astype(jnp.bfloat16)


def _inproj(xs, g, wtok, wfm, cos, sin, pos_tile):
    t = sum(x.shape[0] for x in xs)
    n = t // TOK_TILE
    n_first = xs[0].shape[0] // TOK_TILE
    npb = t // PAIR
    bf = jnp.bfloat16
    tile = lambda w: pl.BlockSpec((TOK_TILE, w), lambda i, p: (i, 0))
    whole = lambda a: pl.BlockSpec(a.shape, lambda i, p: (0,) * a.ndim)
    rope_spec = pl.BlockSpec((TOK_TILE, LANES), lambda i, p: (p[i], 0))
    fm_spec = lambda r: pl.BlockSpec((PAIRS_PER_TILE, r, PAIR), lambda i, p: (i, 0, 0))
    return pl.pallas_call(
        functools.partial(_inproj_kernel, n_first=n_first, n_sources=len(xs)),
        grid_spec=pltpu.PrefetchScalarGridSpec(
            num_scalar_prefetch=1, grid=(n,),
            in_specs=_token_specs(n_first, D_MODEL, len(xs))
                     + [whole(g), whole(wtok), whole(wfm), rope_spec, rope_spec],
            out_specs=[tile(NA_WIDTH), tile(NA_WIDTH), tile(SWA_WIDTH), tile(SWA_KV_WIDTH),
                       tile(SWA_KV_WIDTH), fm_spec(NA_WIDTH), fm_spec(SWA_KV_WIDTH)]),
        out_shape=[jax.ShapeDtypeStruct((t, NA_WIDTH), bf), jax.ShapeDtypeStruct((t, NA_WIDTH), bf),
                   jax.ShapeDtypeStruct((t, SWA_WIDTH), bf), jax.ShapeDtypeStruct((t, SWA_KV_WIDTH), bf),
                   jax.ShapeDtypeStruct((t, SWA_KV_WIDTH), bf),
                   jax.ShapeDtypeStruct((npb, NA_WIDTH, PAIR), bf),
                   jax.ShapeDtypeStruct((npb, SWA_KV_WIDTH, PAIR), bf)],
        compiler_params=pltpu.CompilerParams(dimension_semantics=("arbitrary",),
                                             vmem_limit_bytes=VMEM_LIMIT),
        name="inproj",
    )(pos_tile, *xs, g, wtok, wfm, cos, sin)


def _na_kernel(win_ref, bidx_ref, q_ref, k0_ref, k1_ref, k2_ref, k3_ref,
               v0_ref, v1_ref, v2_ref, v3_ref, bias_ref, o_ref, kbuf, vbuf):
    step = pl.program_id(0)
    half = NA_HALO_PAIRS * PAIR // 4
    for j, (kr, vr) in enumerate(((k0_ref, v0_ref), (k1_ref, v1_ref), (k2_ref, v2_ref), (k3_ref, v3_ref))):
        kbuf[j * half:(j + 1) * half, :] = kr[...]
        vbuf[2 * j:2 * j + 2] = vr[...]
    lane = lax.broadcasted_iota(jnp.int32, (PAIR, LANES), 1)
    low = lane < HEAD_DIM
    zero = jnp.zeros((PAIR, LANES), jnp.bfloat16)

    rels = [win_ref[step * PAIRS_PER_TILE + pp] for pp in range(PAIRS_PER_TILE)]

    def scores(pp, hp):
        cols = slice(hp * LANES, (hp + 1) * LANES)
        qp = q_ref[pp * PAIR:(pp + 1) * PAIR, cols]
        qbd = jnp.concatenate([jnp.where(low, qp, zero), jnp.where(low, zero, qp)], axis=0)
        krow = pl.multiple_of(rels[pp] * PAIR, PAIR)
        kp = kbuf[pl.ds(krow, NA_WIN_ROWS * GRID_W), cols]
        return _dot_nt(kp, qbd)

    def softmax(pp, hp, s):
        p = step * PAIRS_PER_TILE + pp
        pieces = []
        for i in range(NA_WIN_ROWS):
            bi = bidx_ref[p * NA_WIN_ROWS + i]
            b = jnp.concatenate([bias_ref[(2 * hp) * N_BIAS_TILES + bi],
                                 bias_ref[(2 * hp + 1) * N_BIAS_TILES + bi]], axis=1)
            pieces.append(s[i * GRID_W:(i + 1) * GRID_W, :] + b)
        s = jnp.concatenate(pieces, axis=0)
        e = jnp.exp2(s - jnp.max(s, axis=0, keepdims=True))
        return e.astype(jnp.bfloat16), jnp.sum(e, axis=0, keepdims=True)

    def values(pp, hp, e, denom):
        cols = slice(hp * LANES, (hp + 1) * LANES)
        vt = jnp.concatenate([vbuf[rels[pp] + j, cols, :] for j in range(NA_WIN_PAIRS)], axis=1)
        ot = _dot(vt, e) / denom
        o_pair = jnp.concatenate([ot[:HEAD_DIM, :LANES], ot[HEAD_DIM:, LANES:]], axis=0)
        o_ref[pp * PAIR:(pp + 1) * PAIR, cols] = o_pair.T.astype(jnp.bfloat16)

    items = [(pp, hp) for pp in range(PAIRS_PER_TILE) for hp in range(NA_HEADS // 2)]
    s_val, p_val = {}, {}
    for t in range(len(items) + 2):
        if t < len(items):
            s_val[t] = scores(*items[t])
        if 0 <= t - 1 < len(items):
            p_val[t - 1] = softmax(*items[t - 1], s_val.pop(t - 1))
        if 0 <= t - 2 < len(items):
            values(*items[t - 2], *p_val.pop(t - 2))


def _na(qa, ka, vat, bias, win, bidx):
    t = qa.shape[0]
    n = t // TOK_TILE
    hb = NA_HALO_PAIRS * PAIR // 4
    nhb = t // hb
    halo_k = lambda j: pl.BlockSpec((hb, NA_WIDTH), lambda i, w, b: (jnp.clip(2 * i - 1 + j, 0, nhb - 1), 0))
    halo_v = lambda j: pl.BlockSpec((2, NA_WIDTH, PAIR), lambda i, w, b: (jnp.clip(2 * i - 1 + j, 0, nhb - 1), 0, 0))
    tile = pl.BlockSpec((TOK_TILE, NA_WIDTH), lambda i, w, b: (i, 0))
    return pl.pallas_call(
        _na_kernel,
        grid_spec=pltpu.PrefetchScalarGridSpec(
            num_scalar_prefetch=2, grid=(n,),
            in_specs=[tile] + [halo_k(j) for j in range(4)] + [halo_v(j) for j in range(4)]
                     + [pl.BlockSpec(bias.shape, lambda i, w, b: (0, 0, 0))],
            out_specs=tile,
            scratch_shapes=[pltpu.VMEM((NA_HALO_PAIRS * PAIR, NA_WIDTH), jnp.bfloat16),
                            pltpu.VMEM((NA_HALO_PAIRS, NA_WIDTH, PAIR), jnp.bfloat16)]),
        out_shape=jax.ShapeDtypeStruct((t, NA_WIDTH), jnp.bfloat16),
        compiler_params=pltpu.CompilerParams(dimension_semantics=("arbitrary",),
                                             vmem_limit_bytes=VMEM_LIMIT),
        name="na_attn",
    )(win, bidx, qa, ka, ka, ka, ka, vat, vat, vat, vat, bias)


def _swa_kernel(flag_ref, q_ref, kap_ref, kac_ref, kan_ref, kbp_ref, kbc_ref, kbn_ref,
                vp_ref, vc_ref, vn_ref, sink_ref, o_ref, kabuf, kbbuf, vbuf):
    step = pl.program_id(0)
    nblk = TOK_TILE // PAIR
    kabuf[:PAIR] = kap_ref[...]
    kabuf[PAIR:PAIR + TOK_TILE] = kac_ref[...]
    kabuf[PAIR + TOK_TILE:] = kan_ref[...]
    kbbuf[:PAIR] = kbp_ref[...]
    kbbuf[PAIR:PAIR + TOK_TILE] = kbc_ref[...]
    kbbuf[PAIR + TOK_TILE:] = kbn_ref[...]
    vbuf[0:1] = vp_ref[...]
    vbuf[1:1 + nblk] = vc_ref[...]
    vbuf[1 + nblk:] = vn_ref[...]

    nkeys = 3 * PAIR
    ki = lax.broadcasted_iota(jnp.int32, (nkeys, LANES), 0)
    qi = lax.broadcasted_iota(jnp.int32, (nkeys, LANES), 1)
    rel = ki - qi
    band = jnp.where(rel >= 0, jnp.where(rel <= 2 * WINDOW, 0.0, NEG), NEG).astype(jnp.float32)
    in_prev = ki < PAIR
    in_next = ki >= 2 * PAIR
    low = qi < HEAD_DIM
    zero = jnp.zeros((nkeys, LANES), jnp.bfloat16)

    masks = []
    for qb in range(nblk):
        flags = flag_ref[step * nblk + qb]
        pen_prev = (flags & 1).astype(jnp.float32) * NEG
        pen_next = ((flags >> 1) & 1).astype(jnp.float32) * NEG
        mask = band + jnp.where(in_prev, pen_prev, 0.0) + jnp.where(in_next, pen_next, 0.0)
        masks.append(jnp.concatenate([mask, mask], axis=1))

    def scores(qb, kv):
        rows = slice(qb * PAIR, (qb + 1) * PAIR)
        q_even = q_ref[rows, (2 * kv) * LANES:(2 * kv + 1) * LANES]
        q_odd = q_ref[rows, (2 * kv + 1) * LANES:(2 * kv + 2) * LANES]
        qcat = jnp.concatenate([q_even, q_odd], axis=0)
        ka = kabuf[qb * PAIR:qb * PAIR + nkeys, :]
        kb = kbbuf[qb * PAIR:qb * PAIR + nkeys, :]
        k_lo = jnp.where(low, ka if kv == 0 else kb, zero)
        k_hi = jnp.where(low, zero, kb if kv == 0 else ka)
        return _dot_nt(k_lo, qcat), _dot_nt(k_hi, qcat)

    def softmax(qb, kv, s_pair):
        probs, denoms = [], []
        for vi, s in enumerate(s_pair):
            s = s + masks[qb]
            snk = sink_ref[2 * kv + vi:2 * kv + vi + 1, :]
            m = jnp.maximum(jnp.max(s, axis=0, keepdims=True), snk)
            e = jnp.exp2(s - m)
            denoms.append(jnp.sum(e, axis=0, keepdims=True) + jnp.exp2(snk - m))
            probs.append(e.astype(jnp.bfloat16))
        return jnp.concatenate(probs, axis=1), jnp.concatenate(denoms, axis=1)

    def values(qb, kv, pt, denom):
        vt = jnp.concatenate([vbuf[qb + j] for j in range(3)], axis=1)
        ot = _dot(vt[kv * HEAD_DIM:(kv + 1) * HEAD_DIM, :], pt) / denom
        for pair in range(2):
            o_pair = jnp.concatenate([ot[:, pair * LANES:(pair + 1) * LANES],
                                      ot[:, (2 + pair) * LANES:(3 + pair) * LANES]], axis=0)
            o_ref[qb * PAIR:(qb + 1) * PAIR, (2 * kv + pair) * LANES:(2 * kv + pair + 1) * LANES] = (
                o_pair.T.astype(jnp.bfloat16))

    items = [(qb, kv) for qb in range(nblk) for kv in range(SWA_KV_HEADS)]
    s_val, p_val = {}, {}
    for t in range(len(items) + 2):
        if t < len(items):
            s_val[t] = scores(*items[t])
        if 0 <= t - 1 < len(items):
            p_val[t - 1] = softmax(*items[t - 1], s_val.pop(t - 1))
        if 0 <= t - 2 < len(items):
            values(*items[t - 2], *p_val.pop(t - 2))


def _swa(qs, ksa, ksb, vst, sinkvec, flags):
    t = qs.shape[0]
    n = t // TOK_TILE
    nblk = TOK_TILE // PAIR
    npb = t // PAIR
    prev = lambda i, f: (jnp.maximum(nblk * i - 1, 0), 0)
    nxt = lambda i, f: (jnp.minimum(nblk * i + nblk, npb - 1), 0)
    cur = lambda i, f: (i, 0)
    kspecs = [pl.BlockSpec((PAIR, SWA_KV_WIDTH), prev), pl.BlockSpec((TOK_TILE, SWA_KV_WIDTH), cur),
              pl.BlockSpec((PAIR, SWA_KV_WIDTH), nxt)]
    vspecs = [pl.BlockSpec((1, SWA_KV_WIDTH, PAIR), lambda i, f: prev(i, f) + (0,)),
              pl.BlockSpec((nblk, SWA_KV_WIDTH, PAIR), lambda i, f: (i, 0, 0)),
              pl.BlockSpec((1, SWA_KV_WIDTH, PAIR), lambda i, f: nxt(i, f) + (0,))]
    tile = pl.BlockSpec((TOK_TILE, SWA_WIDTH), cur)
    return pl.pallas_call(
        _swa_kernel,
        grid_spec=pltpu.PrefetchScalarGridSpec(
            num_scalar_prefetch=1, grid=(n,),
            in_specs=[tile] + kspecs + kspecs + vspecs + [pl.BlockSpec(sinkvec.shape, lambda i, f: (0, 0))],
            out_specs=tile,
            scratch_shapes=[pltpu.VMEM((TOK_TILE + 2 * PAIR, SWA_KV_WIDTH), jnp.bfloat16),
                            pltpu.VMEM((TOK_TILE + 2 * PAIR, SWA_KV_WIDTH), jnp.bfloat16),
                            pltpu.VMEM((nblk + 2, SWA_KV_WIDTH, PAIR), jnp.bfloat16)]),
        out_shape=jax.ShapeDtypeStruct((t, SWA_WIDTH), jnp.bfloat16),
        compiler_params=pltpu.CompilerParams(dimension_semantics=("arbitrary",),
                                             vmem_limit_bytes=VMEM_LIMIT),
        name="swa_attn",
    )(flags, qs, ksa, ksa, ksa, ksb, ksb, ksb, vst, vst, vst, sinkvec)


def _mlp_kernel(*refs, final, n_first, n_sources):
    x_refs, refs = refs[:n_sources], refs[n_sources:]
    oa_ref, ob_ref, wout_ref, g_ref, wup_ref, wdn_ref, gf_ref = refs[:7]
    out_refs = refs[7:]
    mix = jnp.concatenate([oa_ref[...], ob_ref[...]], axis=1)
    x1 = _load_tokens(x_refs, n_first) + _dot(mix, wout_ref[...])
    h = _rms(x1, g_ref[...]).astype(jnp.bfloat16)
    acc = x1
    for c in range(D_FF // D_MODEL):
        cols = slice(c * D_MODEL, (c + 1) * D_MODEL)
        u = jnp.maximum(_dot(h, wup_ref[:, cols]), 0.0)
        acc = acc + _dot((u * u).astype(jnp.bfloat16), wdn_ref[cols, :])
    if not final:
        out_refs[0][...] = acc
        return
    y = _rms(acc, gf_ref[...])
    step = pl.program_id(0)

    @pl.when(step < n_first)
    def _():
        out_refs[0][...] = y

    @pl.when(step >= n_first)
    def _():
        out_refs[1][...] = y


def _mlp(xs, oa, ob, wout, g, wup, wdn, gf, final, n_first):
    t = oa.shape[0]
    n = t // TOK_TILE
    tile = lambda w: pl.BlockSpec((TOK_TILE, w), lambda i: (i, 0))
    whole = lambda a: pl.BlockSpec(a.shape, lambda i: (0,) * a.ndim, pipeline_mode=pl.Buffered(1))
    if final:
        out_specs = _token_specs(n_first, D_MODEL, 2)
        out_shape = [jax.ShapeDtypeStruct((n_first * TOK_TILE, D_MODEL), jnp.float32),
                     jax.ShapeDtypeStruct((t - n_first * TOK_TILE, D_MODEL), jnp.float32)]
    else:
        out_specs = tile(D_MODEL)
        out_shape = jax.ShapeDtypeStruct((t, D_MODEL), jnp.float32)
    return pl.pallas_call(
        functools.partial(_mlp_kernel, final=final, n_first=n_first, n_sources=len(xs)),
        grid=(n,),
        in_specs=_token_specs(n_first, D_MODEL, len(xs))
                 + [tile(NA_WIDTH), tile(SWA_WIDTH), whole(wout), whole(g), whole(wup), whole(wdn), whole(gf)],
        out_specs=out_specs,
        out_shape=out_shape,
        compiler_params=pltpu.CompilerParams(dimension_semantics=("arbitrary",),
                                             vmem_limit_bytes=VMEM_LIMIT),
        name="mlp",
    )(*xs, oa, ob, wout, g, wup, wdn, gf)


def _static_tables(seq_lens):
    starts = np.concatenate([[0], np.cumsum(seq_lens)[:-1]]).astype(np.int64)
    total = int(np.sum(seq_lens))
    pos_tile = np.zeros(total // TOK_TILE, np.int32)
    win = np.zeros(total // PAIR, np.int32)
    bidx = np.zeros((total // PAIR, NA_WIN_ROWS), np.int32)
    flags = np.zeros(total // PAIR, np.int32)
    for s0, s in zip(starts, seq_lens):
        assert s % TOK_TILE == 0 and s0 % TOK_TILE == 0 and s // GRID_W >= NA_WIN_ROWS
        rows = s // GRID_W
        for i in range(s // TOK_TILE):
            pos_tile[s0 // TOK_TILE + i] = i
        flags[s0 // PAIR] |= 1
        flags[(s0 + s) // PAIR - 1] |= 2
        for q in range(rows // 2):
            p = s0 // PAIR + q
            r0 = 2 * q
            kb = min(max(r0 - NA_ROWS // 2, 0), rows - NA_WIN_ROWS)
            assert kb % 2 == 0
            rel = s0 // PAIR + kb // 2 - ((p // PAIRS_PER_TILE) * PAIRS_PER_TILE - 2)
            assert 0 <= rel and rel + NA_WIN_PAIRS <= NA_HALO_PAIRS
            win[p] = rel
            for i in range(NA_WIN_ROWS):
                kr = kb + i
                ok = []
                for r in (r0, r0 + 1):
                    rs = min(max(r - NA_ROWS // 2, 0), rows - NA_ROWS)
                    ok.append(rs <= kr < rs + NA_ROWS)
                d1 = kr - (r0 + 1) + NA_ROWS - 1
                if ok[0] and ok[1]:
                    assert 0 <= d1 <= 13
                    bidx[p, i] = d1
                elif ok[0]:
                    assert d1 == 2
                    bidx[p, i] = 14
                elif ok[1]:
                    assert d1 == 10
                    bidx[p, i] = 15
                else:
                    bidx[p, i] = 16
    return pos_tile, win, bidx.reshape(-1), flags


def _na_bias_tiles(rpb):
    cols = np.arange(GRID_W)
    cstart = np.clip(cols - NA_COLS // 2, 0, GRID_W - NA_COLS)
    kc = cols[:, None]
    c = cols[None, :]
    col_ok = (kc >= cstart[None, :]) & (kc < cstart[None, :] + NA_COLS)
    dc = np.clip(kc - c + NA_COLS - 1, 0, 2 * NA_COLS - 2)
    tc = jnp.where(col_ok, rpb.astype(jnp.float32)[..., dc] * LOG2E, NEG)
    neg = jnp.full(tc.shape[:2] + (1, GRID_W, GRID_W), NEG, jnp.float32)
    both = jnp.concatenate([tc[:, :, 1:15], tc[:, :, 0:14]], axis=-1)
    first_only = jnp.concatenate([tc[:, :, 3:4], neg], axis=-1)
    second_only = jnp.concatenate([neg, tc[:, :, 10:11]], axis=-1)
    none = jnp.concatenate([neg, neg], axis=-1)
    tiles = jnp.concatenate([both, first_only, second_only, none], axis=2)
    l = tiles.shape[0]
    return tiles.reshape(l, NA_HEADS * N_BIAS_TILES, GRID_W, PAIR)


def _rope_tables(max_len):
    half = HEAD_DIM // 2
    inv = ROPE_THETA ** (-jnp.arange(half, dtype=jnp.float32) / half)
    ang = jnp.arange(max_len, dtype=jnp.float32)[:, None] * inv[None, :]
    cos = jnp.tile(jnp.cos(ang), (1, LANES // half))
    sin = jnp.tile(jnp.sin(ang), (1, LANES // half))
    sign = np.where((np.arange(LANES) % HEAD_DIM) < half, -1.0, 1.0).astype(np.float32)
    return cos, sin * sign[None, :]


def kernel(x_prompt, x_sample, norm_mix, w_in, rpb, sink, w_out, norm_mlp, w_up, w_down, norm_final):
    depth = w_in.shape[0]
    seq_lens = [x_prompt.shape[1]] * x_prompt.shape[0] + [x_sample.shape[1]] * x_sample.shape[0]
    pos_tile, win, bidx, flags = (jnp.asarray(a) for a in _static_tables(seq_lens))
    cos, sin = _rope_tables(max(seq_lens))

    bf = jnp.bfloat16
    scale = HEAD_DIM ** -0.5 * LOG2E
    o1, o2, o3 = NA_WIDTH, 2 * NA_WIDTH, 3 * NA_WIDTH
    o4 = o3 + SWA_WIDTH
    o5 = o4 + SWA_KV_WIDTH
    wtok = jnp.concatenate([w_in[..., :o1] * scale, w_in[..., o1:o2], w_in[..., o3:o4] * scale,
                            w_in[..., o4:o5]], axis=-1).astype(bf)
    wfm = jnp.swapaxes(jnp.concatenate([w_in[..., o2:o3], w_in[..., o5:]], axis=-1), 1, 2).astype(bf)
    wout = w_out.astype(bf)
    wup = w_up.astype(bf)
    wdn = w_down.astype(bf)
    bias = _na_bias_tiles(rpb)
    heads = np.array([[4 * kv + vi, 4 * kv + 2 + vi] for kv in range(SWA_KV_HEADS) for vi in range(2)])
    sinkvec = jnp.repeat(sink.astype(jnp.float32)[:, heads] * LOG2E, PAIR, axis=-1).reshape(depth, 4, 2 * PAIR)
    sinkvec = jnp.concatenate([sinkvec, jnp.zeros_like(sinkvec)], axis=1)

    xs = (x_prompt.reshape(-1, D_MODEL), x_sample.reshape(-1, D_MODEL))
    n_first = xs[0].shape[0] // TOK_TILE
    gf = norm_final.reshape(1, D_MODEL)
    for l in range(depth):
        qa, ka, qs, ksa, ksb, vat, vst = _inproj(xs, norm_mix[l].reshape(1, D_MODEL), wtok[l], wfm[l],
                                                 cos, sin, pos_tile)
        oa = _na(qa, ka, vat, bias[l], win, bidx)
        ob = _swa(qs, ksa, ksb, vst, sinkvec[l], flags)
        out = _mlp(xs, oa, ob, wout[l], norm_mlp[l].reshape(1, D_MODEL), wup[l], wdn[l], gf,
                   final=(l == depth - 1), n_first=n_first)
        xs = (out,)
    y_prompt, y_sample = out
    return (y_prompt.reshape(x_prompt.shape), y_sample.reshape(x_sample.shape))
```

```python
import functools

import numpy as np
import jax
import jax.numpy as jnp
from jax import lax
from jax.experimental import pallas as pl
from jax.experimental.pallas import tpu as pltpu

D_MODEL = 1024
HEAD_DIM = 64
NA_HEADS = 8
SWA_HEADS = 8
SWA_KV_HEADS = 2
NA_WIDTH = NA_HEADS * HEAD_DIM
SWA_WIDTH = SWA_HEADS * HEAD_DIM
SWA_KV_WIDTH = SWA_KV_HEADS * HEAD_DIM
D_FF = 4 * D_MODEL
GRID_W = 64
NA_ROWS = 8
NA_COLS = 16
WINDOW = 128
ROPE_THETA = 10000.0
EPS = 1e-5
NEG = -1e30
LOG2E = 1.4426950408889634

LANES = 128
TOK_TILE = 512
PAIR = 2 * GRID_W
PAIRS_PER_TILE = TOK_TILE // PAIR
ATT_TILE = 1024
ATT_PAIRS = ATT_TILE // PAIR
NA_WIN_ROWS = 10
NA_WIN_PAIRS = NA_WIN_ROWS // 2
NA_EDGE_PAIRS = 2
NA_HALO_PAIRS = ATT_PAIRS + 2 * NA_EDGE_PAIRS
SUBLANES_BF16 = 16
N_BIAS_TILES = 17
VMEM_LIMIT = 56 * 1024 * 1024

TOK_COLS = NA_WIDTH + NA_WIDTH + SWA_WIDTH + SWA_KV_WIDTH
FM_ROWS = NA_WIDTH + SWA_KV_WIDTH


def _rms(x, g):
    return x * lax.rsqrt(jnp.mean(x * x, axis=-1, keepdims=True) + EPS) * g


def _dot_nt(a, b):
    return lax.dot_general(a, b, (((1,), (1,)), ((), ())), preferred_element_type=jnp.float32)


def _dot(a, b):
    return jnp.dot(a, b, preferred_element_type=jnp.float32)


def _token_specs(n_first, width, n_sources):
    if n_sources == 1:
        return [pl.BlockSpec((TOK_TILE, width), lambda i, *_: (i, 0))]
    return [pl.BlockSpec((TOK_TILE, width), lambda i, *_: (jnp.minimum(i, n_first - 1), 0)),
            pl.BlockSpec((TOK_TILE, width), lambda i, *_: (jnp.maximum(i - n_first, 0), 0))]


def _load_tokens(refs, n_first):
    if len(refs) == 1:
        return refs[0][...]
    return jnp.where(pl.program_id(0) < n_first, refs[0][...], refs[1][...])


def _rope(x, cos, sin_signed, first_half):
    rot = jnp.where(first_half, pltpu.roll(x, LANES - HEAD_DIM // 2, 1), pltpu.roll(x, HEAD_DIM // 2, 1))
    return x * cos + rot * sin_signed


def _inproj_kernel(pos_ref, *refs, n_first, n_sources):
    del pos_ref
    x_refs, refs = refs[:n_sources], refs[n_sources:]
    (g_ref, wtok_ref, wfm_ref, cos_ref, sin_ref,
     qa_ref, ka_ref, qs_ref, ksa_ref, ksb_ref, vat_ref, vst_ref) = refs
    h = _rms(_load_tokens(x_refs, n_first), g_ref[...]).astype(jnp.bfloat16)
    proj = _dot(h, wtok_ref[...])
    qa_ref[...] = proj[:, :NA_WIDTH].astype(jnp.bfloat16)
    ka_ref[...] = proj[:, NA_WIDTH:2 * NA_WIDTH].astype(jnp.bfloat16)
    cos = cos_ref[...]
    sin = sin_ref[...]
    lane = lax.broadcasted_iota(jnp.int32, cos.shape, 1)
    first_half = (lane % HEAD_DIM) < (HEAD_DIM // 2)
    base = 2 * NA_WIDTH
    for j in range(SWA_WIDTH // LANES):
        q = proj[:, base + j * LANES: base + (j + 1) * LANES]
        qs_ref[:, j * LANES:(j + 1) * LANES] = _rope(q, cos, sin, first_half).astype(jnp.bfloat16)
    k = _rope(proj[:, base + SWA_WIDTH:], cos, sin, first_half)
    ksa_ref[...] = k.astype(jnp.bfloat16)
    ksb_ref[...] = pltpu.roll(k, HEAD_DIM, 1).astype(jnp.bfloat16)
    fm = _dot_nt(wfm_ref[...], h)
    for j in range(TOK_TILE // PAIR):
        vat_ref[j] = fm[:NA_WIDTH, j * PAIR:(j + 1) * PAIR].astype(jnp.bfloat16)
        vst_ref[j] = fm[NA_WIDTH:, j * PAIR:(j + 1) * PAIR].astype(jnp.bfloat16)


def _inproj(xs, g, wtok, wfm, cos, sin, pos_tile):
    t = sum(x.shape[0] for x in xs)
    n = t // TOK_TILE
    n_first = xs[0].shape[0] // TOK_TILE
    npb = t // PAIR
    bf = jnp.bfloat16
    tile = lambda w: pl.BlockSpec((TOK_TILE, w), lambda i, p: (i, 0))
    whole = lambda a: pl.BlockSpec(a.shape, lambda i, p: (0,) * a.ndim)
    rope_spec = pl.BlockSpec((TOK_TILE, LANES), lambda i, p: (p[i], 0))
    fm_spec = lambda r: pl.BlockSpec((PAIRS_PER_TILE, r, PAIR), lambda i, p: (i, 0, 0))
    return pl.pallas_call(
        functools.partial(_inproj_kernel, n_first=n_first, n_sources=len(xs)),
        grid_spec=pltpu.PrefetchScalarGridSpec(
            num_scalar_prefetch=1, grid=(n,),
            in_specs=_token_specs(n_first, D_MODEL, len(xs))
                     + [whole(g), whole(wtok), whole(wfm), rope_spec, rope_spec],
            out_specs=[tile(NA_WIDTH), tile(NA_WIDTH), tile(SWA_WIDTH), tile(SWA_KV_WIDTH),
                       tile(SWA_KV_WIDTH), fm_spec(NA_WIDTH), fm_spec(SWA_KV_WIDTH)]),
        out_shape=[jax.ShapeDtypeStruct((t, NA_WIDTH), bf), jax.ShapeDtypeStruct((t, NA_WIDTH), bf),
                   jax.ShapeDtypeStruct((t, SWA_WIDTH), bf), jax.ShapeDtypeStruct((t, SWA_KV_WIDTH), bf),
                   jax.ShapeDtypeStruct((t, SWA_KV_WIDTH), bf),
                   jax.ShapeDtypeStruct((npb, NA_WIDTH, PAIR), bf),
                   jax.ShapeDtypeStruct((npb, SWA_KV_WIDTH, PAIR), bf)],
        compiler_params=pltpu.CompilerParams(dimension_semantics=("arbitrary",),
                                             vmem_limit_bytes=VMEM_LIMIT),
        name="inproj",
    )(pos_tile, *xs, g, wtok, wfm, cos, sin)


def _pipeline(items, scores, softmax, values):
    s_val, p_val = {}, {}
    n = len(items)

    def step(t):
        if t < n:
            s_val[t] = scores(*items[t])
        if 0 <= t - 1 < n:
            p_val[t - 1] = softmax(*items[t - 1], s_val.pop(t - 1))
        if 0 <= t - 2 < n:
            values(*items[t - 2], p_val.pop(t - 2))

    return [functools.partial(step, t) for t in range(n + 2)]


def _na_steps(step, win_ref, bidx_ref, q_ref, kp_ref, kc_ref, kn_ref, vp_ref, vc_ref, vn_ref, bias_ref,
              o_ref, kbuf, vbuf):
    edge = NA_EDGE_PAIRS * PAIR
    kbuf[:edge, :] = kp_ref[...]
    kbuf[edge:edge + ATT_TILE, :] = kc_ref[...]
    kbuf[edge + ATT_TILE:, :] = kn_ref[...]
    vbuf[:NA_EDGE_PAIRS] = vp_ref[...]
    vbuf[NA_EDGE_PAIRS:NA_EDGE_PAIRS + ATT_PAIRS] = vc_ref[...]
    vbuf[NA_EDGE_PAIRS + ATT_PAIRS:] = vn_ref[...]
    lane = lax.broadcasted_iota(jnp.int32, (PAIR, LANES), 1)
    low = lane < HEAD_DIM
    zero = jnp.zeros((PAIR, LANES), jnp.bfloat16)

    rels = [win_ref[step * ATT_PAIRS + pp] for pp in range(ATT_PAIRS)]

    def scores(pp, hp):
        cols = slice(hp * LANES, (hp + 1) * LANES)
        qp = q_ref[pp * PAIR:(pp + 1) * PAIR, cols]
        qbd = jnp.concatenate([jnp.where(low, qp, zero), jnp.where(low, zero, qp)], axis=0)
        krow = pl.multiple_of(rels[pp] * PAIR, PAIR)
        kp = kbuf[pl.ds(krow, NA_WIN_ROWS * GRID_W), cols]
        s = _dot_nt(kp, qbd)
        p = step * ATT_PAIRS + pp
        pieces = []
        for i in range(NA_WIN_ROWS):
            bi = bidx_ref[p * NA_WIN_ROWS + i]
            b = jnp.concatenate([bias_ref[(2 * hp) * N_BIAS_TILES + bi],
                                 bias_ref[(2 * hp + 1) * N_BIAS_TILES + bi]], axis=1)
            pieces.append(s[i * GRID_W:(i + 1) * GRID_W, :] + b)
        s = jnp.concatenate(pieces, axis=0)
        return s, jnp.max(s, axis=0, keepdims=True)

    def softmax(pp, hp, sm):
        s, m = sm
        e = jnp.exp2(s - m)
        return e.astype(jnp.bfloat16), jnp.sum(e, axis=0, keepdims=True)

    def values(pp, hp, ed):
        e, denom = ed
        cols = slice(hp * LANES, (hp + 1) * LANES)
        vt = jnp.concatenate([vbuf[rels[pp] + j, cols, :] for j in range(NA_WIN_PAIRS)], axis=1)
        ot = _dot(vt, e) / denom
        o_pair = jnp.concatenate([ot[:HEAD_DIM, :LANES], ot[HEAD_DIM:, LANES:]], axis=0)
        o_ref[pp * PAIR:(pp + 1) * PAIR, cols] = o_pair.T.astype(jnp.bfloat16)

    items = [(pp, hp) for pp in range(ATT_PAIRS) for hp in range(NA_HEADS // 2)]
    return _pipeline(items, scores, softmax, values)


def _swa_steps(step, flag_ref, q_ref, kap_ref, kac_ref, kan_ref, kbp_ref, kbc_ref, kbn_ref,
               vp_ref, vc_ref, vn_ref, sink_ref, o_ref, kabuf, kbbuf, vbuf):
    nblk = ATT_PAIRS
    kabuf[:PAIR] = kap_ref[...]
    kabuf[PAIR:PAIR + ATT_TILE] = kac_ref[...]
    kabuf[PAIR + ATT_TILE:] = kan_ref[...]
    kbbuf[:PAIR] = kbp_ref[...]
    kbbuf[PAIR:PAIR + ATT_TILE] = kbc_ref[...]
    kbbuf[PAIR + ATT_TILE:] = kbn_ref[...]
    vbuf[0:1] = vp_ref[...]
    vbuf[1:1 + nblk] = vc_ref[...]
    vbuf[1 + nblk:] = vn_ref[...]

    nkeys = 3 * PAIR
    ki = lax.broadcasted_iota(jnp.int32, (nkeys, LANES), 0)
    qi = lax.broadcasted_iota(jnp.int32, (nkeys, LANES), 1)
    rel = ki - qi
    band = jnp.where(rel >= 0, jnp.where(rel <= 2 * WINDOW, 0.0, NEG), NEG).astype(jnp.float32)
    in_prev = ki < PAIR
    in_next = ki >= 2 * PAIR
    low = qi < HEAD_DIM
    zero = jnp.zeros((nkeys, LANES), jnp.bfloat16)
    ones = jnp.ones((SUBLANES_BF16, nkeys), jnp.bfloat16)

    masks = []
    for qb in range(nblk):
        flags = flag_ref[step * nblk + qb]
        pen_prev = (flags & 1).astype(jnp.float32) * NEG
        pen_next = ((flags >> 1) & 1).astype(jnp.float32) * NEG
        mask = band + jnp.where(in_prev, pen_prev, 0.0) + jnp.where(in_next, pen_next, 0.0)
        masks.append(jnp.concatenate([mask, mask], axis=1))

    def scores(qb, kv):
        rows = slice(qb * PAIR, (qb + 1) * PAIR)
        q_even = q_ref[rows, (2 * kv) * LANES:(2 * kv + 1) * LANES]
        q_odd = q_ref[rows, (2 * kv + 1) * LANES:(2 * kv + 2) * LANES]
        qcat = jnp.concatenate([q_even, q_odd], axis=0)
        ka = kabuf[qb * PAIR:qb * PAIR + nkeys, :]
        kb = kbbuf[qb * PAIR:qb * PAIR + nkeys, :]
        k_lo = jnp.where(low, ka if kv == 0 else kb, zero)
        k_hi = jnp.where(low, zero, kb if kv == 0 else ka)
        return _dot_nt(k_lo, qcat), _dot_nt(k_hi, qcat)

    def softmax(qb, kv, s_pair):
        probs, sink_terms = [], []
        for vi, s in enumerate(s_pair):
            s = s + masks[qb]
            snk = sink_ref[2 * kv + vi:2 * kv + vi + 1, :]
            m = jnp.maximum(jnp.max(s, axis=0, keepdims=True), snk)
            probs.append(jnp.exp2(s - m).astype(jnp.bfloat16))
            sink_terms.append(jnp.exp2(snk - m))
        return jnp.concatenate(probs, axis=1), jnp.concatenate(sink_terms, axis=1)

    def values(qb, kv, ps):
        pt, sink_term = ps
        vt = jnp.concatenate([vbuf[qb + j] for j in range(3)], axis=1)
        ot = _dot(jnp.concatenate([vt[kv * HEAD_DIM:(kv + 1) * HEAD_DIM, :], ones], axis=0), pt)
        ot = ot[:HEAD_DIM] / (ot[HEAD_DIM:HEAD_DIM + 1] + sink_term)
        for pair in range(2):
            o_pair = jnp.concatenate([ot[:, pair * LANES:(pair + 1) * LANES],
                                      ot[:, (2 + pair) * LANES:(3 + pair) * LANES]], axis=0)
            o_ref[qb * PAIR:(qb + 1) * PAIR, (2 * kv + pair) * LANES:(2 * kv + pair + 1) * LANES] = (
                o_pair.T.astype(jnp.bfloat16))

    items = [(qb, kv) for qb in range(nblk) for kv in range(SWA_KV_HEADS)]
    return _pipeline(items, scores, softmax, values)


def _na_kernel(win_ref, bidx_ref, *refs):
    for run_step in _na_steps(pl.program_id(0), win_ref, bidx_ref, *refs):
        run_step()


def _swa_kernel(flag_ref, *refs):
    for run_step in _swa_steps(pl.program_id(0), flag_ref, *refs):
        run_step()


def _attn(qa, ka, vat, bias, win, bidx, qs, ksa, ksb, vst, sinkvec, flags):
    t = qa.shape[0]
    n = t // ATT_TILE
    npb = t // PAIR
    edge = NA_EDGE_PAIRS * PAIR
    per = ATT_TILE // edge
    na_prev = lambda i, *_: jnp.maximum(per * i - 1, 0)
    na_next = lambda i, *_: jnp.minimum(per * i + per, t // edge - 1)
    sw_prev = lambda i, *_: jnp.maximum(ATT_PAIRS * i - 1, 0)
    sw_next = lambda i, *_: jnp.minimum(ATT_PAIRS * i + ATT_PAIRS, npb - 1)
    cur = lambda i, *_: i

    def rows(block, width, idx):
        return pl.BlockSpec((block, width), lambda *a: (idx(*a), 0))

    def slabs(block, width, idx):
        return pl.BlockSpec((block, width, PAIR), lambda *a: (idx(*a), 0, 0))

    whole = lambda a: pl.BlockSpec(a.shape, lambda *_: (0,) * a.ndim)
    na_specs = ([rows(ATT_TILE, NA_WIDTH, cur)]
                + [rows(edge, NA_WIDTH, na_prev), rows(ATT_TILE, NA_WIDTH, cur), rows(edge, NA_WIDTH, na_next)]
                + [slabs(NA_EDGE_PAIRS, NA_WIDTH, na_prev), slabs(ATT_PAIRS, NA_WIDTH, cur),
                   slabs(NA_EDGE_PAIRS, NA_WIDTH, na_next)]
                + [whole(bias)])
    sw_k = [rows(PAIR, SWA_KV_WIDTH, sw_prev), rows(ATT_TILE, SWA_KV_WIDTH, cur), rows(PAIR, SWA_KV_WIDTH, sw_next)]
    swa_specs = ([rows(ATT_TILE, SWA_WIDTH, cur)] + sw_k + sw_k
                 + [slabs(1, SWA_KV_WIDTH, sw_prev), slabs(ATT_PAIRS, SWA_KV_WIDTH, cur),
                    slabs(1, SWA_KV_WIDTH, sw_next)]
                 + [whole(sinkvec)])
    bf = jnp.bfloat16
    params = pltpu.CompilerParams(dimension_semantics=("arbitrary",), vmem_limit_bytes=VMEM_LIMIT)
    oa = pl.pallas_call(
        _na_kernel,
        grid_spec=pltpu.PrefetchScalarGridSpec(
            num_scalar_prefetch=2, grid=(n,),
            in_specs=na_specs,
            out_specs=rows(ATT_TILE, NA_WIDTH, cur),
            scratch_shapes=[pltpu.VMEM((NA_HALO_PAIRS * PAIR, NA_WIDTH), bf),
                            pltpu.VMEM((NA_HALO_PAIRS, NA_WIDTH, PAIR), bf)]),
        out_shape=jax.ShapeDtypeStruct((t, NA_WIDTH), bf),
        compiler_params=params,
        name="na_attn",
    )(win, bidx, qa, ka, ka, ka, vat, vat, vat, bias)
    ob = pl.pallas_call(
        _swa_kernel,
        grid_spec=pltpu.PrefetchScalarGridSpec(
            num_scalar_prefetch=1, grid=(n,),
            in_specs=swa_specs,
            out_specs=rows(ATT_TILE, SWA_WIDTH, cur),
            scratch_shapes=[pltpu.VMEM((ATT_TILE + 2 * PAIR, SWA_KV_WIDTH), bf),
                            pltpu.VMEM((ATT_TILE + 2 * PAIR, SWA_KV_WIDTH), bf),
                            pltpu.VMEM((ATT_PAIRS + 2, SWA_KV_WIDTH, PAIR), bf)]),
        out_shape=jax.ShapeDtypeStruct((t, SWA_WIDTH), bf),
        compiler_params=params,
        name="swa_attn",
    )(flags, qs, ksa, ksa, ksa, ksb, ksb, ksb, vst, vst, vst, sinkvec)
    return oa, ob


def _mlp_kernel(*refs, final, n_first, n_sources):
    x_refs, refs = refs[:n_sources], refs[n_sources:]
    oa_ref, ob_ref, wout_ref, g_ref, wup_ref, wdn_ref, gf_ref = refs[:7]
    out_refs = refs[7:]
    mix = jnp.concatenate([oa_ref[...], ob_ref[...]], axis=1)
    x1 = _load_tokens(x_refs, n_first) + _dot(mix, wout_ref[...])
    h = _rms(x1, g_ref[...]).astype(jnp.bfloat16)
    acc = x1
    for c in range(D_FF // D_MODEL):
        cols = slice(c * D_MODEL, (c + 1) * D_MODEL)
        u = jnp.maximum(_dot(h, wup_ref[:, cols]), 0.0)
        acc = acc + _dot((u * u).astype(jnp.bfloat16), wdn_ref[cols, :])
    if not final:
        out_refs[0][...] = acc
        return
    y = _rms(acc, gf_ref[...])
    step = pl.program_id(0)

    @pl.when(step < n_first)
    def _():
        out_refs[0][...] = y

    @pl.when(step >= n_first)
    def _():
        out_refs[1][...] = y


def _mlp(xs, oa, ob, wout, g, wup, wdn, gf, final, n_first):
    t = oa.shape[0]
    n = t // TOK_TILE
    tile = lambda w: pl.BlockSpec((TOK_TILE, w), lambda i: (i, 0))
    whole = lambda a: pl.BlockSpec(a.shape, lambda i: (0,) * a.ndim, pipeline_mode=pl.Buffered(1))
    if final:
        out_specs = _token_specs(n_first, D_MODEL, 2)
        out_shape = [jax.ShapeDtypeStruct((n_first * TOK_TILE, D_MODEL), jnp.float32),
                     jax.ShapeDtypeStruct((t - n_first * TOK_TILE, D_MODEL), jnp.float32)]
    else:
        out_specs = tile(D_MODEL)
        out_shape = jax.ShapeDtypeStruct((t, D_MODEL), jnp.float32)
    return pl.pallas_call(
        functools.partial(_mlp_kernel, final=final, n_first=n_first, n_sources=len(xs)),
        grid=(n,),
        in_specs=_token_specs(n_first, D_MODEL, len(xs))
                 + [tile(NA_WIDTH), tile(SWA_WIDTH), whole(wout), whole(g), whole(wup), whole(wdn), whole(gf)],
        out_specs=out_specs,
        out_shape=out_shape,
        compiler_params=pltpu.CompilerParams(dimension_semantics=("arbitrary",),
                                             vmem_limit_bytes=VMEM_LIMIT),
        name="mlp",
    )(*xs, oa, ob, wout, g, wup, wdn, gf)


def _static_tables(seq_lens):
    starts = np.concatenate([[0], np.cumsum(seq_lens)[:-1]]).astype(np.int64)
    total = int(np.sum(seq_lens))
    pos_tile = np.zeros(total // TOK_TILE, np.int32)
    win = np.zeros(total // PAIR, np.int32)
    bidx = np.zeros((total // PAIR, NA_WIN_ROWS), np.int32)
    flags = np.zeros(total // PAIR, np.int32)
    for s0, s in zip(starts, seq_lens):
        assert s % ATT_TILE == 0 and s0 % ATT_TILE == 0 and s // GRID_W >= NA_WIN_ROWS
        rows = s // GRID_W
        for i in range(s // TOK_TILE):
            pos_tile[s0 // TOK_TILE + i] = i
        flags[s0 // PAIR] |= 1
        flags[(s0 + s) // PAIR - 1] |= 2
        for q in range(rows // 2):
            p = s0 // PAIR + q
            r0 = 2 * q
            kb = min(max(r0 - NA_ROWS // 2, 0), rows - NA_WIN_ROWS)
            assert kb % 2 == 0
            rel = s0 // PAIR + kb // 2 - ((p // ATT_PAIRS) * ATT_PAIRS - NA_EDGE_PAIRS)
            assert 0 <= rel and rel + NA_WIN_PAIRS <= NA_HALO_PAIRS
            win[p] = rel
            for i in range(NA_WIN_ROWS):
                kr = kb + i
                ok = []
                for r in (r0, r0 + 1):
                    rs = min(max(r - NA_ROWS // 2, 0), rows - NA_ROWS)
                    ok.append(rs <= kr < rs + NA_ROWS)
                d1 = kr - (r0 + 1) + NA_ROWS - 1
                if ok[0] and ok[1]:
                    assert 0 <= d1 <= 13
                    bidx[p, i] = d1
                elif ok[0]:
                    assert d1 == 2
                    bidx[p, i] = 14
                elif ok[1]:
                    assert d1 == 10
                    bidx[p, i] = 15
                else:
                    bidx[p, i] = 16
    return pos_tile, win, bidx.reshape(-1), flags


def _na_bias_tiles(rpb):
    cols = np.arange(GRID_W)
    cstart = np.clip(cols - NA_COLS // 2, 0, GRID_W - NA_COLS)
    kc = cols[:, None]
    c = cols[None, :]
    col_ok = (kc >= cstart[None, :]) & (kc < cstart[None, :] + NA_COLS)
    dc = np.clip(kc - c + NA_COLS - 1, 0, 2 * NA_COLS - 2)
    onehot = ((np.arange(2 * NA_COLS - 1)[:, None, None] == dc[None]) & col_ok[None]).astype(np.float32)
    tc = jnp.einsum('lhdk,kn->lhdn', rpb.astype(jnp.float32) * LOG2E, onehot.reshape(2 * NA_COLS - 1, -1),
                    precision=lax.Precision.HIGHEST)
    tc = tc.reshape(tc.shape[:3] + (GRID_W, GRID_W)) + np.where(col_ok, 0.0, NEG).astype(np.float32)
    neg = jnp.full(tc.shape[:2] + (1, GRID_W, GRID_W), NEG, jnp.float32)
    both = jnp.concatenate([tc[:, :, 1:15], tc[:, :, 0:14]], axis=-1)
    first_only = jnp.concatenate([tc[:, :, 3:4], neg], axis=-1)
    second_only = jnp.concatenate([neg, tc[:, :, 10:11]], axis=-1)
    none = jnp.concatenate([neg, neg], axis=-1)
    tiles = jnp.concatenate([both, first_only, second_only, none], axis=2)
    l = tiles.shape[0]
    return tiles.reshape(l, NA_HEADS * N_BIAS_TILES, GRID_W, PAIR)


def _rope_tables(max_len):
    half = HEAD_DIM // 2
    inv = ROPE_THETA ** (-jnp.arange(half, dtype=jnp.float32) / half)
    ang = jnp.arange(max_len, dtype=jnp.float32)[:, None] * inv[None, :]
    cos = jnp.tile(jnp.cos(ang), (1, LANES // half))
    sin = jnp.tile(jnp.sin(ang), (1, LANES // half))
    sign = np.where((np.arange(LANES) % HEAD_DIM) < half, -1.0, 1.0).astype(np.float32)
    return cos, sin * sign[None, :]


def kernel(x_prompt, x_sample, norm_mix, w_in, rpb, sink, w_out, norm_mlp, w_up, w_down, norm_final):
    depth = w_in.shape[0]
    seq_lens = [x_prompt.shape[1]] * x_prompt.shape[0] + [x_sample.shape[1]] * x_sample.shape[0]
    pos_tile, win, bidx, flags = (jnp.asarray(a) for a in _static_tables(seq_lens))
    cos, sin = _rope_tables(max(seq_lens))

    bf = jnp.bfloat16
    scale = HEAD_DIM ** -0.5 * LOG2E
    o1, o2, o3 = NA_WIDTH, 2 * NA_WIDTH, 3 * NA_WIDTH
    o4 = o3 + SWA_WIDTH
    o5 = o4 + SWA_KV_WIDTH
    wtok = jnp.concatenate([w_in[..., :o1] * scale, w_in[..., o1:o2], w_in[..., o3:o4] * scale,
                            w_in[..., o4:o5]], axis=-1).astype(bf)
    wfm = jnp.swapaxes(jnp.concatenate([w_in[..., o2:o3], w_in[..., o5:]], axis=-1), 1, 2).astype(bf)
    wout = w_out.astype(bf)
    wup = w_up.astype(bf)
    wdn = w_down.astype(bf)
    bias = _na_bias_tiles(rpb)
    heads = np.array([[4 * kv + vi, 4 * kv + 2 + vi] for kv in range(SWA_KV_HEADS) for vi in range(2)])
    sinkvec = jnp.repeat(sink.astype(jnp.float32)[:, heads] * LOG2E, PAIR, axis=-1).reshape(depth, 4, 2 * PAIR)
    sinkvec = jnp.concatenate([sinkvec, jnp.zeros_like(sinkvec)], axis=1)

    xs = (x_prompt.reshape(-1, D_MODEL), x_sample.reshape(-1, D_MODEL))
    n_first = xs[0].shape[0] // TOK_TILE
    gf = norm_final.reshape(1, D_MODEL)
    for l in range(depth):
        qa, ka, qs, ksa, ksb, vat, vst = _inproj(xs, norm_mix[l].reshape(1, D_MODEL), wtok[l], wfm[l],
                                                 cos, sin, pos_tile)
        oa, ob = _attn(qa, ka, vat, bias[l], win, bidx, qs, ksa, ksb, vst, sinkvec[l], flags)
        out = _mlp(xs, oa, ob, wout[l], norm_mlp[l].reshape(1, D_MODEL), wup[l], wdn[l], gf,
                   final=(l == depth - 1), n_first=n_first)
        xs = (out,)
    y_prompt, y_sample = out
    return (y_prompt.reshape(x_prompt.shape), y_sample.reshape(x_sample.shape))
```

```python
import functools

import numpy as np
import jax
import jax.numpy as jnp
from jax import lax
from jax.experimental import pallas as pl
from jax.experimental.pallas import tpu as pltpu

D_MODEL = 1024
HEAD_DIM = 64
NA_HEADS = 8
SWA_HEADS = 8
SWA_KV_HEADS = 2
NA_WIDTH = NA_HEADS * HEAD_DIM
SWA_WIDTH = SWA_HEADS * HEAD_DIM
SWA_KV_WIDTH = SWA_KV_HEADS * HEAD_DIM
D_FF = 4 * D_MODEL
GRID_W = 64
NA_ROWS = 8
NA_COLS = 16
WINDOW = 128
ROPE_THETA = 10000.0
EPS = 1e-5
NEG = -1e30
LOG2E = 1.4426950408889634

LANES = 128
TOK_TILE = 512
PAIR = 2 * GRID_W
PAIRS_PER_TILE = TOK_TILE // PAIR
ATT_TILE = 1024
ATT_PAIRS = ATT_TILE // PAIR
NA_WIN_ROWS = 9
NA_WIN_PAIRS = (NA_WIN_ROWS + 1) // 2
NA_EDGE_PAIRS = 2
NA_HALO_PAIRS = ATT_PAIRS + 2 * NA_EDGE_PAIRS
SUBLANES_BF16 = 16
N_BIAS_TILES = 17
VMEM_LIMIT = 56 * 1024 * 1024

TOK_COLS = NA_WIDTH + NA_WIDTH + SWA_WIDTH + SWA_KV_WIDTH
FM_ROWS = NA_WIDTH + SWA_KV_WIDTH


def _rms(x, g):
    return x * lax.rsqrt(jnp.mean(x * x, axis=-1, keepdims=True) + EPS) * g


def _dot_nt(a, b):
    return lax.dot_general(a, b, (((1,), (1,)), ((), ())), preferred_element_type=jnp.float32)


def _dot(a, b):
    return jnp.dot(a, b, preferred_element_type=jnp.float32)


def _token_specs(n_first, width, n_sources):
    if n_sources == 1:
        return [pl.BlockSpec((TOK_TILE, width), lambda i, *_: (i, 0))]
    return [pl.BlockSpec((TOK_TILE, width), lambda i, *_: (jnp.minimum(i, n_first - 1), 0)),
            pl.BlockSpec((TOK_TILE, width), lambda i, *_: (jnp.maximum(i - n_first, 0), 0))]


def _load_tokens(refs, n_first):
    if len(refs) == 1:
        return refs[0][...]
    return jnp.where(pl.program_id(0) < n_first, refs[0][...], refs[1][...])


def _rope(x, cos, sin_signed, first_half):
    rot = jnp.where(first_half, pltpu.roll(x, LANES - HEAD_DIM // 2, 1), pltpu.roll(x, HEAD_DIM // 2, 1))
    return x * cos + rot * sin_signed


def _inproj_kernel(pos_ref, *refs, n_first, n_sources):
    del pos_ref
    x_refs, refs = refs[:n_sources], refs[n_sources:]
    (g_ref, wtok_ref, wfm_ref, cos_ref, sin_ref,
     qa_ref, ka_ref, qs_ref, ksa_ref, ksb_ref, vat_ref, vst_ref) = refs
    h = _rms(_load_tokens(x_refs, n_first), g_ref[...]).astype(jnp.bfloat16)
    proj = _dot(h, wtok_ref[...])
    qa_ref[...] = proj[:, :NA_WIDTH].astype(jnp.bfloat16)
    ka_ref[...] = proj[:, NA_WIDTH:2 * NA_WIDTH].astype(jnp.bfloat16)
    cos = cos_ref[...]
    sin = sin_ref[...]
    lane = lax.broadcasted_iota(jnp.int32, cos.shape, 1)
    first_half = (lane % HEAD_DIM) < (HEAD_DIM // 2)
    base = 2 * NA_WIDTH
    for j in range(SWA_WIDTH // LANES):
        q = proj[:, base + j * LANES: base + (j + 1) * LANES]
        qs_ref[:, j * LANES:(j + 1) * LANES] = _rope(q, cos, sin, first_half).astype(jnp.bfloat16)
    k = _rope(proj[:, base + SWA_WIDTH:], cos, sin, first_half)
    ksa_ref[...] = k.astype(jnp.bfloat16)
    ksb_ref[...] = pltpu.roll(k, HEAD_DIM, 1).astype(jnp.bfloat16)
    fm = _dot_nt(wfm_ref[...], h)
    for j in range(TOK_TILE // PAIR):
        vat_ref[j] = fm[:NA_WIDTH, j * PAIR:(j + 1) * PAIR].astype(jnp.bfloat16)
        vst_ref[j] = fm[NA_WIDTH:, j * PAIR:(j + 1) * PAIR].astype(jnp.bfloat16)


def _inproj(xs, g, wtok, wfm, cos, sin, pos_tile):
    t = sum(x.shape[0] for x in xs)
    n = t // TOK_TILE
    n_first = xs[0].shape[0] // TOK_TILE
    npb = t // PAIR
    bf = jnp.bfloat16
    tile = lambda w: pl.BlockSpec((TOK_TILE, w), lambda i, p: (i, 0))
    whole = lambda a: pl.BlockSpec(a.shape, lambda i, p: (0,) * a.ndim)
    rope_spec = pl.BlockSpec((TOK_TILE, LANES), lambda i, p: (p[i], 0))
    fm_spec = lambda r: pl.BlockSpec((PAIRS_PER_TILE, r, PAIR), lambda i, p: (i, 0, 0))
    return pl.pallas_call(
        functools.partial(_inproj_kernel, n_first=n_first, n_sources=len(xs)),
        grid_spec=pltpu.PrefetchScalarGridSpec(
            num_scalar_prefetch=1, grid=(n,),
            in_specs=_token_specs(n_first, D_MODEL, len(xs))
                     + [whole(g), whole(wtok), whole(wfm), rope_spec, rope_spec],
            out_specs=[tile(NA_WIDTH), tile(NA_WIDTH), tile(SWA_WIDTH), tile(SWA_KV_WIDTH),
                       tile(SWA_KV_WIDTH), fm_spec(NA_WIDTH), fm_spec(SWA_KV_WIDTH)]),
        out_shape=[jax.ShapeDtypeStruct((t, NA_WIDTH), bf), jax.ShapeDtypeStruct((t, NA_WIDTH), bf),
                   jax.ShapeDtypeStruct((t, SWA_WIDTH), bf), jax.ShapeDtypeStruct((t, SWA_KV_WIDTH), bf),
                   jax.ShapeDtypeStruct((t, SWA_KV_WIDTH), bf),
                   jax.ShapeDtypeStruct((npb, NA_WIDTH, PAIR), bf),
                   jax.ShapeDtypeStruct((npb, SWA_KV_WIDTH, PAIR), bf)],
        compiler_params=pltpu.CompilerParams(dimension_semantics=("arbitrary",),
                                             vmem_limit_bytes=VMEM_LIMIT),
        name="inproj",
    )(pos_tile, *xs, g, wtok, wfm, cos, sin)


def _pipeline(items, scores, softmax, values):
    s_val, p_val = {}, {}
    n = len(items)

    def step(t):
        if t < n:
            s_val[t] = scores(*items[t])
        if 0 <= t - 1 < n:
            p_val[t - 1] = softmax(*items[t - 1], s_val.pop(t - 1))
        if 0 <= t - 2 < n:
            values(*items[t - 2], p_val.pop(t - 2))

    return [functools.partial(step, t) for t in range(n + 2)]


def _na_steps(step, win_ref, bidx_ref, q_ref, kp_ref, kc_ref, kn_ref, vp_ref, vc_ref, vn_ref, bias_ref,
              o_ref, kbuf, vbuf):
    edge = NA_EDGE_PAIRS * PAIR
    kbuf[:edge, :] = kp_ref[...]
    kbuf[edge:edge + ATT_TILE, :] = kc_ref[...]
    kbuf[edge + ATT_TILE:, :] = kn_ref[...]
    vbuf[:NA_EDGE_PAIRS] = vp_ref[...]
    vbuf[NA_EDGE_PAIRS:NA_EDGE_PAIRS + ATT_PAIRS] = vc_ref[...]
    vbuf[NA_EDGE_PAIRS + ATT_PAIRS:] = vn_ref[...]
    lane = lax.broadcasted_iota(jnp.int32, (PAIR, LANES), 1)
    low = lane < HEAD_DIM
    zero = jnp.zeros((PAIR, LANES), jnp.bfloat16)

    rels = [win_ref[step * ATT_PAIRS + pp] for pp in range(ATT_PAIRS)]

    def scores(pp, hp):
        cols = slice(hp * LANES, (hp + 1) * LANES)
        qp = q_ref[pp * PAIR:(pp + 1) * PAIR, cols]
        qbd = jnp.concatenate([jnp.where(low, qp, zero), jnp.where(low, zero, qp)], axis=0)
        krow = pl.multiple_of(rels[pp] * PAIR, PAIR)
        kp = kbuf[pl.ds(krow, NA_WIN_ROWS * GRID_W), cols]
        s = _dot_nt(kp, qbd)
        p = step * ATT_PAIRS + pp
        pieces = []
        for i in range(NA_WIN_ROWS):
            bi = bidx_ref[p * NA_WIN_ROWS + i]
            b = jnp.concatenate([bias_ref[(2 * hp) * N_BIAS_TILES + bi],
                                 bias_ref[(2 * hp + 1) * N_BIAS_TILES + bi]], axis=1)
            pieces.append(s[i * GRID_W:(i + 1) * GRID_W, :] + b)
        s = jnp.concatenate(pieces, axis=0)
        return s, jnp.max(s, axis=0, keepdims=True)

    def softmax(pp, hp, sm):
        s, m = sm
        e = jnp.exp2(s - m)
        pad = jnp.zeros((NA_WIN_PAIRS * PAIR - NA_WIN_ROWS * GRID_W, 2 * LANES), jnp.bfloat16)
        return jnp.concatenate([e.astype(jnp.bfloat16), pad], axis=0), jnp.sum(e, axis=0, keepdims=True)

    def values(pp, hp, ed):
        e, denom = ed
        cols = slice(hp * LANES, (hp + 1) * LANES)
        vt = jnp.concatenate([vbuf[rels[pp] + j, cols, :] for j in range(NA_WIN_PAIRS)], axis=1)
        ot = _dot(vt, e) / denom
        o_pair = jnp.concatenate([ot[:HEAD_DIM, :LANES], ot[HEAD_DIM:, LANES:]], axis=0)
        o_ref[pp * PAIR:(pp + 1) * PAIR, cols] = o_pair.T.astype(jnp.bfloat16)

    items = [(pp, hp) for pp in range(ATT_PAIRS) for hp in range(NA_HEADS // 2)]
    return _pipeline(items, scores, softmax, values)


def _swa_steps(step, flag_ref, q_ref, kap_ref, kac_ref, kan_ref, kbp_ref, kbc_ref, kbn_ref,
               vp_ref, vc_ref, vn_ref, sink_ref, o_ref, kabuf, kbbuf, vbuf):
    nblk = ATT_PAIRS
    kabuf[:PAIR] = kap_ref[...]
    kabuf[PAIR:PAIR + ATT_TILE] = kac_ref[...]
    kabuf[PAIR + ATT_TILE:] = kan_ref[...]
    kbbuf[:PAIR] = kbp_ref[...]
    kbbuf[PAIR:PAIR + ATT_TILE] = kbc_ref[...]
    kbbuf[PAIR + ATT_TILE:] = kbn_ref[...]
    vbuf[0:1] = vp_ref[...]
    vbuf[1:1 + nblk] = vc_ref[...]
    vbuf[1 + nblk:] = vn_ref[...]

    nkeys = 3 * PAIR
    ki = lax.broadcasted_iota(jnp.int32, (nkeys, LANES), 0)
    qi = lax.broadcasted_iota(jnp.int32, (nkeys, LANES), 1)
    rel = ki - qi
    band = jnp.where(rel >= 0, jnp.where(rel <= 2 * WINDOW, 0.0, NEG), NEG).astype(jnp.float32)
    in_prev = ki < PAIR
    in_next = ki >= 2 * PAIR
    low = qi < HEAD_DIM
    zero = jnp.zeros((nkeys, LANES), jnp.bfloat16)
    ones = jnp.ones((SUBLANES_BF16, nkeys), jnp.bfloat16)

    masks = []
    for qb in range(nblk):
        flags = flag_ref[step * nblk + qb]
        pen_prev = (flags & 1).astype(jnp.float32) * NEG
        pen_next = ((flags >> 1) & 1).astype(jnp.float32) * NEG
        mask = band + jnp.where(in_prev, pen_prev, 0.0) + jnp.where(in_next, pen_next, 0.0)
        masks.append((jnp.concatenate([mask[:PAIR], mask[:PAIR]], axis=1),
                      jnp.concatenate([mask[2 * PAIR:], mask[2 * PAIR:]], axis=1)))

    def scores(qb, kv):
        rows = slice(qb * PAIR, (qb + 1) * PAIR)
        q_even = q_ref[rows, (2 * kv) * LANES:(2 * kv + 1) * LANES]
        q_odd = q_ref[rows, (2 * kv + 1) * LANES:(2 * kv + 2) * LANES]
        qcat = jnp.concatenate([q_even, q_odd], axis=0)
        ka = kabuf[qb * PAIR:qb * PAIR + nkeys, :]
        kb = kbbuf[qb * PAIR:qb * PAIR + nkeys, :]
        k_lo = jnp.where(low, ka if kv == 0 else kb, zero)
        k_hi = jnp.where(low, zero, kb if kv == 0 else ka)
        return _dot_nt(k_lo, qcat), _dot_nt(k_hi, qcat)

    def softmax(qb, kv, s_pair):
        probs, sink_terms = [], []
        for vi, s in enumerate(s_pair):
            s = jnp.concatenate([s[:PAIR] + masks[qb][0], s[PAIR:2 * PAIR], s[2 * PAIR:] + masks[qb][1]], axis=0)
            snk = sink_ref[2 * kv + vi:2 * kv + vi + 1, :]
            m = jnp.maximum(jnp.max(s, axis=0, keepdims=True), snk)
            probs.append(jnp.exp2(s - m).astype(jnp.bfloat16))
            sink_terms.append(jnp.exp2(snk - m))
        return jnp.concatenate(probs, axis=1), jnp.concatenate(sink_terms, axis=1)

    def values(qb, kv, ps):
        pt, sink_term = ps
        vt = jnp.concatenate([vbuf[qb + j] for j in range(3)], axis=1)
        ot = _dot(jnp.concatenate([vt[kv * HEAD_DIM:(kv + 1) * HEAD_DIM, :], ones], axis=0), pt)
        ot = ot[:HEAD_DIM] / (ot[HEAD_DIM:HEAD_DIM + 1] + sink_term)
        for pair in range(2):
            o_pair = jnp.concatenate([ot[:, pair * LANES:(pair + 1) * LANES],
                                      ot[:, (2 + pair) * LANES:(3 + pair) * LANES]], axis=0)
            o_ref[qb * PAIR:(qb + 1) * PAIR, (2 * kv + pair) * LANES:(2 * kv + pair + 1) * LANES] = (
                o_pair.T.astype(jnp.bfloat16))

    items = [(qb, kv) for qb in range(nblk) for kv in range(SWA_KV_HEADS)]
    return _pipeline(items, scores, softmax, values)


def _na_kernel(win_ref, bidx_ref, *refs):
    for run_step in _na_steps(pl.program_id(0), win_ref, bidx_ref, *refs):
        run_step()


def _swa_kernel(flag_ref, *refs):
    for run_step in _swa_steps(pl.program_id(0), flag_ref, *refs):
        run_step()


def _attn(qa, ka, vat, bias, win, bidx, qs, ksa, ksb, vst, sinkvec, flags):
    t = qa.shape[0]
    n = t // ATT_TILE
    npb = t // PAIR
    edge = NA_EDGE_PAIRS * PAIR
    per = ATT_TILE // edge
    na_prev = lambda i, *_: jnp.maximum(per * i - 1, 0)
    na_next = lambda i, *_: jnp.minimum(per * i + per, t // edge - 1)
    sw_prev = lambda i, *_: jnp.maximum(ATT_PAIRS * i - 1, 0)
    sw_next = lambda i, *_: jnp.minimum(ATT_PAIRS * i + ATT_PAIRS, npb - 1)
    cur = lambda i, *_: i

    def rows(block, width, idx):
        return pl.BlockSpec((block, width), lambda *a: (idx(*a), 0))

    def slabs(block, width, idx):
        return pl.BlockSpec((block, width, PAIR), lambda *a: (idx(*a), 0, 0))

    whole = lambda a: pl.BlockSpec(a.shape, lambda *_: (0,) * a.ndim)
    na_specs = ([rows(ATT_TILE, NA_WIDTH, cur)]
                + [rows(edge, NA_WIDTH, na_prev), rows(ATT_TILE, NA_WIDTH, cur), rows(edge, NA_WIDTH, na_next)]
                + [slabs(NA_EDGE_PAIRS, NA_WIDTH, na_prev), slabs(ATT_PAIRS, NA_WIDTH, cur),
                   slabs(NA_EDGE_PAIRS, NA_WIDTH, na_next)]
                + [whole(bias)])
    sw_k = [rows(PAIR, SWA_KV_WIDTH, sw_prev), rows(ATT_TILE, SWA_KV_WIDTH, cur), rows(PAIR, SWA_KV_WIDTH, sw_next)]
    swa_specs = ([rows(ATT_TILE, SWA_WIDTH, cur)] + sw_k + sw_k
                 + [slabs(1, SWA_KV_WIDTH, sw_prev), slabs(ATT_PAIRS, SWA_KV_WIDTH, cur),
                    slabs(1, SWA_KV_WIDTH, sw_next)]
                 + [whole(sinkvec)])
    bf = jnp.bfloat16
    params = pltpu.CompilerParams(dimension_semantics=("arbitrary",), vmem_limit_bytes=VMEM_LIMIT)
    oa = pl.pallas_call(
        _na_kernel,
        grid_spec=pltpu.PrefetchScalarGridSpec(
            num_scalar_prefetch=2, grid=(n,),
            in_specs=na_specs,
            out_specs=rows(ATT_TILE, NA_WIDTH, cur),
            scratch_shapes=[pltpu.VMEM((NA_HALO_PAIRS * PAIR, NA_WIDTH), bf),
                            pltpu.VMEM((NA_HALO_PAIRS, NA_WIDTH, PAIR), bf)]),
        out_shape=jax.ShapeDtypeStruct((t, NA_WIDTH), bf),
        compiler_params=params,
        name="na_attn",
    )(win, bidx, qa, ka, ka, ka, vat, vat, vat, bias)
    ob = pl.pallas_call(
        _swa_kernel,
        grid_spec=pltpu.PrefetchScalarGridSpec(
            num_scalar_prefetch=1, grid=(n,),
            in_specs=swa_specs,
            out_specs=rows(ATT_TILE, SWA_WIDTH, cur),
            scratch_shapes=[pltpu.VMEM((ATT_TILE + 2 * PAIR, SWA_KV_WIDTH), bf),
                            pltpu.VMEM((ATT_TILE + 2 * PAIR, SWA_KV_WIDTH), bf),
                            pltpu.VMEM((ATT_PAIRS + 2, SWA_KV_WIDTH, PAIR), bf)]),
        out_shape=jax.ShapeDtypeStruct((t, SWA_WIDTH), bf),
        compiler_params=params,
        name="swa_attn",
    )(flags, qs, ksa, ksa, ksa, ksb, ksb, ksb, vst, vst, vst, sinkvec)
    return oa, ob


def _mlp_kernel(*refs, final, n_first, n_sources):
    x_refs, refs = refs[:n_sources], refs[n_sources:]
    oa_ref, ob_ref, wout_ref, g_ref, wup_ref, wdn_ref, gf_ref = refs[:7]
    out_refs = refs[7:]
    mix = jnp.concatenate([oa_ref[...], ob_ref[...]], axis=1)
    x1 = _load_tokens(x_refs, n_first) + _dot(mix, wout_ref[...])
    h = _rms(x1, g_ref[...]).astype(jnp.bfloat16)
    acc = x1
    for c in range(D_FF // D_MODEL):
        cols = slice(c * D_MODEL, (c + 1) * D_MODEL)
        u = jnp.maximum(_dot(h, wup_ref[:, cols]), 0.0)
        acc = acc + _dot((u * u).astype(jnp.bfloat16), wdn_ref[cols, :])
    if not final:
        out_refs[0][...] = acc
        return
    y = _rms(acc, gf_ref[...])
    step = pl.program_id(0)

    @pl.when(step < n_first)
    def _():
        out_refs[0][...] = y

    @pl.when(step >= n_first)
    def _():
        out_refs[1][...] = y


def _mlp(xs, oa, ob, wout, g, wup, wdn, gf, final, n_first):
    t = oa.shape[0]
    n = t // TOK_TILE
    tile = lambda w: pl.BlockSpec((TOK_TILE, w), lambda i: (i, 0))
    whole = lambda a: pl.BlockSpec(a.shape, lambda i: (0,) * a.ndim, pipeline_mode=pl.Buffered(1))
    if final:
        out_specs = _token_specs(n_first, D_MODEL, 2)
        out_shape = [jax.ShapeDtypeStruct((n_first * TOK_TILE, D_MODEL), jnp.float32),
                     jax.ShapeDtypeStruct((t - n_first * TOK_TILE, D_MODEL), jnp.float32)]
    else:
        out_specs = tile(D_MODEL)
        out_shape = jax.ShapeDtypeStruct((t, D_MODEL), jnp.float32)
    return pl.pallas_call(
        functools.partial(_mlp_kernel, final=final, n_first=n_first, n_sources=len(xs)),
        grid=(n,),
        in_specs=_token_specs(n_first, D_MODEL, len(xs))
                 + [tile(NA_WIDTH), tile(SWA_WIDTH), whole(wout), whole(g), whole(wup), whole(wdn), whole(gf)],
        out_specs=out_specs,
        out_shape=out_shape,
        compiler_params=pltpu.CompilerParams(dimension_semantics=("arbitrary",),
                                             vmem_limit_bytes=VMEM_LIMIT),
        name="mlp",
    )(*xs, oa, ob, wout, g, wup, wdn, gf)


def _static_tables(seq_lens):
    starts = np.concatenate([[0], np.cumsum(seq_lens)[:-1]]).astype(np.int64)
    total = int(np.sum(seq_lens))
    pos_tile = np.zeros(total // TOK_TILE, np.int32)
    win = np.zeros(total // PAIR, np.int32)
    bidx = np.zeros((total // PAIR, NA_WIN_ROWS), np.int32)
    flags = np.zeros(total // PAIR, np.int32)
    for s0, s in zip(starts, seq_lens):
        assert s % ATT_TILE == 0 and s0 % ATT_TILE == 0
        rows = s // GRID_W
        for i in range(s // TOK_TILE):
            pos_tile[s0 // TOK_TILE + i] = i
        flags[s0 // PAIR] |= 1
        flags[(s0 + s) // PAIR - 1] |= 2
        for q in range(rows // 2):
            p = s0 // PAIR + q
            r0 = 2 * q
            kb = min(max(r0 - NA_ROWS // 2, 0), rows - NA_ROWS)
            assert kb % 2 == 0
            rel = s0 // PAIR + kb // 2 - ((p // ATT_PAIRS) * ATT_PAIRS - NA_EDGE_PAIRS)
            assert 0 <= rel and rel + NA_WIN_PAIRS <= NA_HALO_PAIRS
            win[p] = rel
            for i in range(NA_WIN_ROWS):
                kr = kb + i
                ok = []
                for r in (r0, r0 + 1):
                    rs = min(max(r - NA_ROWS // 2, 0), rows - NA_ROWS)
                    ok.append(rs <= kr < rs + NA_ROWS)
                d1 = kr - (r0 + 1) + NA_ROWS - 1
                if ok[0] and ok[1]:
                    assert 0 <= d1 <= 13
                    bidx[p, i] = d1
                elif ok[0]:
                    assert d1 == 2
                    bidx[p, i] = 14
                elif ok[1]:
                    assert d1 == 10
                    bidx[p, i] = 15
                else:
                    bidx[p, i] = 16
    return pos_tile, win, bidx.reshape(-1), flags


def _na_bias_tiles(rpb):
    cols = np.arange(GRID_W)
    cstart = np.clip(cols - NA_COLS // 2, 0, GRID_W - NA_COLS)
    kc = cols[:, None]
    c = cols[None, :]
    col_ok = (kc >= cstart[None, :]) & (kc < cstart[None, :] + NA_COLS)
    dc = np.clip(kc - c + NA_COLS - 1, 0, 2 * NA_COLS - 2)
    onehot = ((np.arange(2 * NA_COLS - 1)[:, None, None] == dc[None]) & col_ok[None]).astype(np.float32)
    tc = jnp.einsum('lhdk,kn->lhdn', rpb.astype(jnp.float32) * LOG2E, onehot.reshape(2 * NA_COLS - 1, -1),
                    precision=lax.Precision.HIGHEST)
    tc = tc.reshape(tc.shape[:3] + (GRID_W, GRID_W)) + np.where(col_ok, 0.0, NEG).astype(np.float32)
    neg = jnp.full(tc.shape[:2] + (1, GRID_W, GRID_W), NEG, jnp.float32)
    both = jnp.concatenate([tc[:, :, 1:15], tc[:, :, 0:14]], axis=-1)
    first_only = jnp.concatenate([tc[:, :, 3:4], neg], axis=-1)
    second_only = jnp.concatenate([neg, tc[:, :, 10:11]], axis=-1)
    none = jnp.concatenate([neg, neg], axis=-1)
    tiles = jnp.concatenate([both, first_only, second_only, none], axis=2)
    l = tiles.shape[0]
    return tiles.reshape(l, NA_HEADS * N_BIAS_TILES, GRID_W, PAIR)


def _rope_tables(max_len):
    half = HEAD_DIM // 2
    inv = ROPE_THETA ** (-jnp.arange(half, dtype=jnp.float32) / half)
    ang = jnp.arange(max_len, dtype=jnp.float32)[:, None] * inv[None, :]
    cos = jnp.tile(jnp.cos(ang), (1, LANES // half))
    sin = jnp.tile(jnp.sin(ang), (1, LANES // half))
    sign = np.where((np.arange(LANES) % HEAD_DIM) < half, -1.0, 1.0).astype(np.float32)
    return cos, sin * sign[None, :]


def kernel(x_prompt, x_sample, norm_mix, w_in, rpb, sink, w_out, norm_mlp, w_up, w_down, norm_final):
    depth = w_in.shape[0]
    seq_lens = [x_prompt.shape[1]] * x_prompt.shape[0] + [x_sample.shape[1]] * x_sample.shape[0]
    pos_tile, win, bidx, flags = (jnp.asarray(a) for a in _static_tables(seq_lens))
    cos, sin = _rope_tables(max(seq_lens))

    bf = jnp.bfloat16
    scale = HEAD_DIM ** -0.5 * LOG2E
    o1, o2, o3 = NA_WIDTH, 2 * NA_WIDTH, 3 * NA_WIDTH
    o4 = o3 + SWA_WIDTH
    o5 = o4 + SWA_KV_WIDTH
    wtok = jnp.concatenate([w_in[..., :o1] * scale, w_in[..., o1:o2], w_in[..., o3:o4] * scale,
                            w_in[..., o4:o5]], axis=-1).astype(bf)
    wfm = jnp.swapaxes(jnp.concatenate([w_in[..., o2:o3], w_in[..., o5:]], axis=-1).astype(bf), 1, 2)
    wout = w_out.astype(bf)
    wup = w_up.astype(bf)
    wdn = w_down.astype(bf)
    bias = _na_bias_tiles(rpb)
    heads = np.array([[4 * kv + vi, 4 * kv + 2 + vi] for kv in range(SWA_KV_HEADS) for vi in range(2)])
    sinkvec = jnp.repeat(sink.astype(jnp.float32)[:, heads] * LOG2E, PAIR, axis=-1).reshape(depth, 4, 2 * PAIR)
    sinkvec = jnp.concatenate([sinkvec, jnp.zeros_like(sinkvec)], axis=1)

    xs = (x_prompt.reshape(-1, D_MODEL), x_sample.reshape(-1, D_MODEL))
    n_first = xs[0].shape[0] // TOK_TILE
    gf = norm_final.reshape(1, D_MODEL)
    for l in range(depth):
        qa, ka, qs, ksa, ksb, vat, vst = _inproj(xs, norm_mix[l].reshape(1, D_MODEL), wtok[l], wfm[l],
                                                 cos, sin, pos_tile)
        oa, ob = _attn(qa, ka, vat, bias[l], win, bidx, qs, ksa, ksb, vst, sinkvec[l], flags)
        out = _mlp(xs, oa, ob, wout[l], norm_mlp[l].reshape(1, D_MODEL), wup[l], wdn[l], gf,
                   final=(l == depth - 1), n_first=n_first)
        xs = (out,)
    y_prompt, y_sample = out
    return (y_prompt.reshape(x_prompt.shape), y_sample.reshape(x_sample.shape))
```

```python
import functools

import numpy as np
import jax
import jax.numpy as jnp
from jax import lax
from jax.experimental import pallas as pl
from jax.experimental.pallas import tpu as pltpu

D_MODEL = 1024
HEAD_DIM = 64
NA_HEADS = 8
SWA_HEADS = 8
SWA_KV_HEADS = 2
NA_WIDTH = NA_HEADS * HEAD_DIM
SWA_WIDTH = SWA_HEADS * HEAD_DIM
SWA_KV_WIDTH = SWA_KV_HEADS * HEAD_DIM
D_FF = 4 * D_MODEL
GRID_W = 64
NA_ROWS = 8
NA_COLS = 16
WINDOW = 128
ROPE_THETA = 10000.0
EPS = 1e-5
NEG = -1e30
LOG2E = 1.4426950408889634

LANES = 128
TOK_TILE = 512
PAIR = 2 * GRID_W
PAIRS_PER_TILE = TOK_TILE // PAIR
ATT_TILE = 2048
ATT_PAIRS = ATT_TILE // PAIR
NA_WIN_ROWS = 9
NA_WIN_PAIRS = (NA_WIN_ROWS + 1) // 2
NA_EDGE_PAIRS = 2
NA_HALO_PAIRS = ATT_PAIRS + 2 * NA_EDGE_PAIRS
SUBLANES_BF16 = 16
N_BIAS_TILES = 17
VMEM_LIMIT = 56 * 1024 * 1024

TOK_COLS = NA_WIDTH + NA_WIDTH + SWA_WIDTH + SWA_KV_WIDTH
FM_ROWS = NA_WIDTH + SWA_KV_WIDTH


def _rms(x, g):
    return x * lax.rsqrt(jnp.mean(x * x, axis=-1, keepdims=True) + EPS) * g


def _dot_nt(a, b):
    return lax.dot_general(a, b, (((1,), (1,)), ((), ())), preferred_element_type=jnp.float32)


def _dot(a, b):
    return jnp.dot(a, b, preferred_element_type=jnp.float32)


def _token_specs(n_first, width, n_sources):
    if n_sources == 1:
        return [pl.BlockSpec((TOK_TILE, width), lambda i, *_: (i, 0))]
    return [pl.BlockSpec((TOK_TILE, width), lambda i, *_: (jnp.minimum(i, n_first - 1), 0)),
            pl.BlockSpec((TOK_TILE, width), lambda i, *_: (jnp.maximum(i - n_first, 0), 0))]


def _load_tokens(refs, n_first):
    if len(refs) == 1:
        return refs[0][...]
    return jnp.where(pl.program_id(0) < n_first, refs[0][...], refs[1][...])


def _rope(x, cos, sin_signed, first_half):
    rot = jnp.where(first_half, pltpu.roll(x, LANES - HEAD_DIM // 2, 1), pltpu.roll(x, HEAD_DIM // 2, 1))
    return x * cos + rot * sin_signed


def _inproj_kernel(pos_ref, *refs, n_first, n_sources):
    del pos_ref
    x_refs, refs = refs[:n_sources], refs[n_sources:]
    (g_ref, wtok_ref, wfm_ref, cos_ref, sin_ref,
     qa_ref, ka_ref, qs_ref, ksa_ref, ksb_ref, vat_ref, vst_ref) = refs
    h = _rms(_load_tokens(x_refs, n_first), g_ref[...]).astype(jnp.bfloat16)
    proj = _dot(h, wtok_ref[...])
    qa_ref[...] = proj[:, :NA_WIDTH].astype(jnp.bfloat16)
    ka_ref[...] = proj[:, NA_WIDTH:2 * NA_WIDTH].astype(jnp.bfloat16)
    cos = cos_ref[...]
    sin = sin_ref[...]
    lane = lax.broadcasted_iota(jnp.int32, cos.shape, 1)
    first_half = (lane % HEAD_DIM) < (HEAD_DIM // 2)
    base = 2 * NA_WIDTH
    for j in range(SWA_WIDTH // LANES):
        q = proj[:, base + j * LANES: base + (j + 1) * LANES]
        qs_ref[:, j * LANES:(j + 1) * LANES] = _rope(q, cos, sin, first_half).astype(jnp.bfloat16)
    k = _rope(proj[:, base + SWA_WIDTH:], cos, sin, first_half)
    ksa_ref[...] = k.astype(jnp.bfloat16)
    ksb_ref[...] = pltpu.roll(k, HEAD_DIM, 1).astype(jnp.bfloat16)
    fm = _dot_nt(wfm_ref[...], h)
    for j in range(TOK_TILE // PAIR):
        vat_ref[j] = fm[:NA_WIDTH, j * PAIR:(j + 1) * PAIR].astype(jnp.bfloat16)
        vst_ref[j] = fm[NA_WIDTH:, j * PAIR:(j + 1) * PAIR].astype(jnp.bfloat16)


def _inproj(xs, g, wtok, wfm, cos, sin, pos_tile):
    t = sum(x.shape[0] for x in xs)
    n = t // TOK_TILE
    n_first = xs[0].shape[0] // TOK_TILE
    npb = t // PAIR
    bf = jnp.bfloat16
    tile = lambda w: pl.BlockSpec((TOK_TILE, w), lambda i, p: (i, 0))
    whole = lambda a: pl.BlockSpec(a.shape, lambda i, p: (0,) * a.ndim)
    rope_spec = pl.BlockSpec((TOK_TILE, LANES), lambda i, p: (p[i], 0))
    fm_spec = lambda r: pl.BlockSpec((PAIRS_PER_TILE, r, PAIR), lambda i, p: (i, 0, 0))
    return pl.pallas_call(
        functools.partial(_inproj_kernel, n_first=n_first, n_sources=len(xs)),
        grid_spec=pltpu.PrefetchScalarGridSpec(
            num_scalar_prefetch=1, grid=(n,),
            in_specs=_token_specs(n_first, D_MODEL, len(xs))
                     + [whole(g), whole(wtok), whole(wfm), rope_spec, rope_spec],
            out_specs=[tile(NA_WIDTH), tile(NA_WIDTH), tile(SWA_WIDTH), tile(SWA_KV_WIDTH),
                       tile(SWA_KV_WIDTH), fm_spec(NA_WIDTH), fm_spec(SWA_KV_WIDTH)]),
        out_shape=[jax.ShapeDtypeStruct((t, NA_WIDTH), bf), jax.ShapeDtypeStruct((t, NA_WIDTH), bf),
                   jax.ShapeDtypeStruct((t, SWA_WIDTH), bf), jax.ShapeDtypeStruct((t, SWA_KV_WIDTH), bf),
                   jax.ShapeDtypeStruct((t, SWA_KV_WIDTH), bf),
                   jax.ShapeDtypeStruct((npb, NA_WIDTH, PAIR), bf),
                   jax.ShapeDtypeStruct((npb, SWA_KV_WIDTH, PAIR), bf)],
        compiler_params=pltpu.CompilerParams(dimension_semantics=("arbitrary",),
                                             vmem_limit_bytes=VMEM_LIMIT),
        name="inproj",
    )(pos_tile, *xs, g, wtok, wfm, cos, sin)


def _pipeline(items, scores, softmax, values):
    s_val, p_val = {}, {}
    n = len(items)

    def step(t):
        if t < n:
            s_val[t] = scores(*items[t])
        if 0 <= t - 1 < n:
            p_val[t - 1] = softmax(*items[t - 1], s_val.pop(t - 1))
        if 0 <= t - 2 < n:
            values(*items[t - 2], p_val.pop(t - 2))

    return [functools.partial(step, t) for t in range(n + 2)]


def _na_steps(step, win_ref, bidx_ref, q_ref, kp_ref, kc_ref, kn_ref, vp_ref, vc_ref, vn_ref, bias_ref,
              o_ref, kbuf, vbuf):
    edge = NA_EDGE_PAIRS * PAIR
    kbuf[:edge, :] = kp_ref[...]
    kbuf[edge:edge + ATT_TILE, :] = kc_ref[...]
    kbuf[edge + ATT_TILE:, :] = kn_ref[...]
    vbuf[:NA_EDGE_PAIRS] = vp_ref[...]
    vbuf[NA_EDGE_PAIRS:NA_EDGE_PAIRS + ATT_PAIRS] = vc_ref[...]
    vbuf[NA_EDGE_PAIRS + ATT_PAIRS:] = vn_ref[...]
    lane = lax.broadcasted_iota(jnp.int32, (PAIR, LANES), 1)
    low = lane < HEAD_DIM
    zero = jnp.zeros((PAIR, LANES), jnp.bfloat16)

    rels = [win_ref[step * ATT_PAIRS + pp] for pp in range(ATT_PAIRS)]

    def scores(pp, hp):
        cols = slice(hp * LANES, (hp + 1) * LANES)
        qp = q_ref[pp * PAIR:(pp + 1) * PAIR, cols]
        qbd = jnp.concatenate([jnp.where(low, qp, zero), jnp.where(low, zero, qp)], axis=0)
        krow = pl.multiple_of(rels[pp] * PAIR, PAIR)
        kp = kbuf[pl.ds(krow, NA_WIN_ROWS * GRID_W), cols]
        s = _dot_nt(kp, qbd)
        p = step * ATT_PAIRS + pp
        pieces = []
        for i in range(NA_WIN_ROWS):
            bi = bidx_ref[p * NA_WIN_ROWS + i]
            b = jnp.concatenate([bias_ref[(2 * hp) * N_BIAS_TILES + bi],
                                 bias_ref[(2 * hp + 1) * N_BIAS_TILES + bi]], axis=1)
            pieces.append(s[i * GRID_W:(i + 1) * GRID_W, :] + b)
        s = jnp.concatenate(pieces, axis=0)
        return s, jnp.max(s, axis=0, keepdims=True)

    def softmax(pp, hp, sm):
        s, m = sm
        e = jnp.exp2(s - m)
        pad = jnp.zeros((NA_WIN_PAIRS * PAIR - NA_WIN_ROWS * GRID_W, 2 * LANES), jnp.bfloat16)
        return jnp.concatenate([e.astype(jnp.bfloat16), pad], axis=0), jnp.sum(e, axis=0, keepdims=True)

    def values(pp, hp, ed):
        e, denom = ed
        cols = slice(hp * LANES, (hp + 1) * LANES)
        vt = jnp.concatenate([vbuf[rels[pp] + j, cols, :] for j in range(NA_WIN_PAIRS)], axis=1)
        ot = _dot(vt, e) / denom
        o_pair = jnp.concatenate([ot[:HEAD_DIM, :LANES], ot[HEAD_DIM:, LANES:]], axis=0)
        o_ref[pp * PAIR:(pp + 1) * PAIR, cols] = o_pair.T.astype(jnp.bfloat16)

    items = [(pp, hp) for pp in range(ATT_PAIRS) for hp in range(NA_HEADS // 2)]
    return _pipeline(items, scores, softmax, values)


def _swa_steps(step, flag_ref, q_ref, kap_ref, kac_ref, kan_ref, kbp_ref, kbc_ref, kbn_ref,
               vp_ref, vc_ref, vn_ref, sink_ref, o_ref, kabuf, kbbuf, vbuf):
    nblk = ATT_PAIRS
    kabuf[:PAIR] = kap_ref[...]
    kabuf[PAIR:PAIR + ATT_TILE] = kac_ref[...]
    kabuf[PAIR + ATT_TILE:] = kan_ref[...]
    kbbuf[:PAIR] = kbp_ref[...]
    kbbuf[PAIR:PAIR + ATT_TILE] = kbc_ref[...]
    kbbuf[PAIR + ATT_TILE:] = kbn_ref[...]
    vbuf[0:1] = vp_ref[...]
    vbuf[1:1 + nblk] = vc_ref[...]
    vbuf[1 + nblk:] = vn_ref[...]

    nkeys = 3 * PAIR
    ki = lax.broadcasted_iota(jnp.int32, (nkeys, LANES), 0)
    qi = lax.broadcasted_iota(jnp.int32, (nkeys, LANES), 1)
    rel = ki - qi
    band = jnp.where(rel >= 0, jnp.where(rel <= 2 * WINDOW, 0.0, NEG), NEG).astype(jnp.float32)
    in_prev = ki < PAIR
    in_next = ki >= 2 * PAIR
    low = qi < HEAD_DIM
    zero = jnp.zeros((nkeys, LANES), jnp.bfloat16)
    ones = jnp.ones((SUBLANES_BF16, nkeys), jnp.bfloat16)

    masks = []
    for qb in range(nblk):
        flags = flag_ref[step * nblk + qb]
        pen_prev = (flags & 1).astype(jnp.float32) * NEG
        pen_next = ((flags >> 1) & 1).astype(jnp.float32) * NEG
        mask = band + jnp.where(in_prev, pen_prev, 0.0) + jnp.where(in_next, pen_next, 0.0)
        masks.append((jnp.concatenate([mask[:PAIR], mask[:PAIR]], axis=1),
                      jnp.concatenate([mask[2 * PAIR:], mask[2 * PAIR:]], axis=1)))

    def scores(qb, kv):
        rows = slice(qb * PAIR, (qb + 1) * PAIR)
        q_even = q_ref[rows, (2 * kv) * LANES:(2 * kv + 1) * LANES]
        q_odd = q_ref[rows, (2 * kv + 1) * LANES:(2 * kv + 2) * LANES]
        qcat = jnp.concatenate([q_even, q_odd], axis=0)
        ka = kabuf[qb * PAIR:qb * PAIR + nkeys, :]
        kb = kbbuf[qb * PAIR:qb * PAIR + nkeys, :]
        k_lo = jnp.where(low, ka if kv == 0 else kb, zero)
        k_hi = jnp.where(low, zero, kb if kv == 0 else ka)
        return _dot_nt(k_lo, qcat), _dot_nt(k_hi, qcat)

    def softmax(qb, kv, s_pair):
        probs, sink_terms = [], []
        for vi, s in enumerate(s_pair):
            s = jnp.concatenate([s[:PAIR] + masks[qb][0], s[PAIR:2 * PAIR], s[2 * PAIR:] + masks[qb][1]], axis=0)
            snk = sink_ref[2 * kv + vi:2 * kv + vi + 1, :]
            m = jnp.maximum(jnp.max(s, axis=0, keepdims=True), snk)
            probs.append(jnp.exp2(s - m).astype(jnp.bfloat16))
            sink_terms.append(jnp.exp2(snk - m))
        return jnp.concatenate(probs, axis=1), jnp.concatenate(sink_terms, axis=1)

    def values(qb, kv, ps):
        pt, sink_term = ps
        vt = jnp.concatenate([vbuf[qb + j] for j in range(3)], axis=1)
        ot = _dot(jnp.concatenate([vt[kv * HEAD_DIM:(kv + 1) * HEAD_DIM, :], ones], axis=0), pt)
        ot = ot[:HEAD_DIM] / (ot[HEAD_DIM:HEAD_DIM + 1] + sink_term)
        for pair in range(2):
            o_pair = jnp.concatenate([ot[:, pair * LANES:(pair + 1) * LANES],
                                      ot[:, (2 + pair) * LANES:(3 + pair) * LANES]], axis=0)
            o_ref[qb * PAIR:(qb + 1) * PAIR, (2 * kv + pair) * LANES:(2 * kv + pair + 1) * LANES] = (
                o_pair.T.astype(jnp.bfloat16))

    items = [(qb, kv) for qb in range(nblk) for kv in range(SWA_KV_HEADS)]
    return _pipeline(items, scores, softmax, values)


def _na_kernel(win_ref, bidx_ref, *refs):
    for run_step in _na_steps(pl.program_id(0), win_ref, bidx_ref, *refs):
        run_step()


def _swa_kernel(flag_ref, *refs):
    for run_step in _swa_steps(pl.program_id(0), flag_ref, *refs):
        run_step()


def _attn(qa, ka, vat, bias, win, bidx, qs, ksa, ksb, vst, sinkvec, flags):
    t = qa.shape[0]
    n = t // ATT_TILE
    npb = t // PAIR
    edge = NA_EDGE_PAIRS * PAIR
    per = ATT_TILE // edge
    na_prev = lambda i, *_: jnp.maximum(per * i - 1, 0)
    na_next = lambda i, *_: jnp.minimum(per * i + per, t // edge - 1)
    sw_prev = lambda i, *_: jnp.maximum(ATT_PAIRS * i - 1, 0)
    sw_next = lambda i, *_: jnp.minimum(ATT_PAIRS * i + ATT_PAIRS, npb - 1)
    cur = lambda i, *_: i

    def rows(block, width, idx):
        return pl.BlockSpec((block, width), lambda *a: (idx(*a), 0))

    def slabs(block, width, idx):
        return pl.BlockSpec((block, width, PAIR), lambda *a: (idx(*a), 0, 0))

    whole = lambda a: pl.BlockSpec(a.shape, lambda *_: (0,) * a.ndim)
    na_specs = ([rows(ATT_TILE, NA_WIDTH, cur)]
                + [rows(edge, NA_WIDTH, na_prev), rows(ATT_TILE, NA_WIDTH, cur), rows(edge, NA_WIDTH, na_next)]
                + [slabs(NA_EDGE_PAIRS, NA_WIDTH, na_prev), slabs(ATT_PAIRS, NA_WIDTH, cur),
                   slabs(NA_EDGE_PAIRS, NA_WIDTH, na_next)]
                + [whole(bias)])
    sw_k = [rows(PAIR, SWA_KV_WIDTH, sw_prev), rows(ATT_TILE, SWA_KV_WIDTH, cur), rows(PAIR, SWA_KV_WIDTH, sw_next)]
    swa_specs = ([rows(ATT_TILE, SWA_WIDTH, cur)] + sw_k + sw_k
                 + [slabs(1, SWA_KV_WIDTH, sw_prev), slabs(ATT_PAIRS, SWA_KV_WIDTH, cur),
                    slabs(1, SWA_KV_WIDTH, sw_next)]
                 + [whole(sinkvec)])
    bf = jnp.bfloat16
    params = pltpu.CompilerParams(dimension_semantics=("arbitrary",), vmem_limit_bytes=VMEM_LIMIT)
    oa = pl.pallas_call(
        _na_kernel,
        grid_spec=pltpu.PrefetchScalarGridSpec(
            num_scalar_prefetch=2, grid=(n,),
            in_specs=na_specs,
            out_specs=rows(ATT_TILE, NA_WIDTH, cur),
            scratch_shapes=[pltpu.VMEM((NA_HALO_PAIRS * PAIR, NA_WIDTH), bf),
                            pltpu.VMEM((NA_HALO_PAIRS, NA_WIDTH, PAIR), bf)]),
        out_shape=jax.ShapeDtypeStruct((t, NA_WIDTH), bf),
        compiler_params=params,
        name="na_attn",
    )(win, bidx, qa, ka, ka, ka, vat, vat, vat, bias)
    ob = pl.pallas_call(
        _swa_kernel,
        grid_spec=pltpu.PrefetchScalarGridSpec(
            num_scalar_prefetch=1, grid=(n,),
            in_specs=swa_specs,
            out_specs=rows(ATT_TILE, SWA_WIDTH, cur),
            scratch_shapes=[pltpu.VMEM((ATT_TILE + 2 * PAIR, SWA_KV_WIDTH), bf),
                            pltpu.VMEM((ATT_TILE + 2 * PAIR, SWA_KV_WIDTH), bf),
                            pltpu.VMEM((ATT_PAIRS + 2, SWA_KV_WIDTH, PAIR), bf)]),
        out_shape=jax.ShapeDtypeStruct((t, SWA_WIDTH), bf),
        compiler_params=params,
        name="swa_attn",
    )(flags, qs, ksa, ksa, ksa, ksb, ksb, ksb, vst, vst, vst, sinkvec)
    return oa, ob


def _mlp_kernel(*refs, final, n_first, n_sources):
    x_refs, refs = refs[:n_sources], refs[n_sources:]
    oa_ref, ob_ref, wout_ref, g_ref, wup_ref, wdn_ref, gf_ref = refs[:7]
    out_refs = refs[7:]
    mix = jnp.concatenate([oa_ref[...], ob_ref[...]], axis=1)
    x1 = _load_tokens(x_refs, n_first) + _dot(mix, wout_ref[...])
    h = _rms(x1, g_ref[...]).astype(jnp.bfloat16)
    acc = x1
    for c in range(D_FF // D_MODEL):
        cols = slice(c * D_MODEL, (c + 1) * D_MODEL)
        u = jnp.maximum(_dot(h, wup_ref[:, cols]), 0.0)
        acc = acc + _dot((u * u).astype(jnp.bfloat16), wdn_ref[cols, :])
    if not final:
        out_refs[0][...] = acc
        return
    y = _rms(acc, gf_ref[...])
    step = pl.program_id(0)

    @pl.when(step < n_first)
    def _():
        out_refs[0][...] = y

    @pl.when(step >= n_first)
    def _():
        out_refs[1][...] = y


def _mlp(xs, oa, ob, wout, g, wup, wdn, gf, final, n_first):
    t = oa.shape[0]
    n = t // TOK_TILE
    tile = lambda w: pl.BlockSpec((TOK_TILE, w), lambda i: (i, 0))
    whole = lambda a: pl.BlockSpec(a.shape, lambda i: (0,) * a.ndim, pipeline_mode=pl.Buffered(1))
    if final:
        out_specs = _token_specs(n_first, D_MODEL, 2)
        out_shape = [jax.ShapeDtypeStruct((n_first * TOK_TILE, D_MODEL), jnp.float32),
                     jax.ShapeDtypeStruct((t - n_first * TOK_TILE, D_MODEL), jnp.float32)]
    else:
        out_specs = tile(D_MODEL)
        out_shape = jax.ShapeDtypeStruct((t, D_MODEL), jnp.float32)
    return pl.pallas_call(
        functools.partial(_mlp_kernel, final=final, n_first=n_first, n_sources=len(xs)),
        grid=(n,),
        in_specs=_token_specs(n_first, D_MODEL, len(xs))
                 + [tile(NA_WIDTH), tile(SWA_WIDTH), whole(wout), whole(g), whole(wup), whole(wdn), whole(gf)],
        out_specs=out_specs,
        out_shape=out_shape,
        compiler_params=pltpu.CompilerParams(dimension_semantics=("arbitrary",),
                                             vmem_limit_bytes=VMEM_LIMIT),
        name="mlp",
    )(*xs, oa, ob, wout, g, wup, wdn, gf)


def _static_tables(seq_lens):
    starts = np.concatenate([[0], np.cumsum(seq_lens)[:-1]]).astype(np.int64)
    total = int(np.sum(seq_lens))
    pos_tile = np.zeros(total // TOK_TILE, np.int32)
    win = np.zeros(total // PAIR, np.int32)
    bidx = np.zeros((total // PAIR, NA_WIN_ROWS), np.int32)
    flags = np.zeros(total // PAIR, np.int32)
    for s0, s in zip(starts, seq_lens):
        assert s % ATT_TILE == 0 and s0 % ATT_TILE == 0
        rows = s // GRID_W
        for i in range(s // TOK_TILE):
            pos_tile[s0 // TOK_TILE + i] = i
        flags[s0 // PAIR] |= 1
        flags[(s0 + s) // PAIR - 1] |= 2
        for q in range(rows // 2):
            p = s0 // PAIR + q
            r0 = 2 * q
            kb = min(max(r0 - NA_ROWS // 2, 0), rows - NA_ROWS)
            assert kb % 2 == 0
            rel = s0 // PAIR + kb // 2 - ((p // ATT_PAIRS) * ATT_PAIRS - NA_EDGE_PAIRS)
            assert 0 <= rel and rel + NA_WIN_PAIRS <= NA_HALO_PAIRS
            win[p] = rel
            for i in range(NA_WIN_ROWS):
                kr = kb + i
                ok = []
                for r in (r0, r0 + 1):
                    rs = min(max(r - NA_ROWS // 2, 0), rows - NA_ROWS)
                    ok.append(rs <= kr < rs + NA_ROWS)
                d1 = kr - (r0 + 1) + NA_ROWS - 1
                if ok[0] and ok[1]:
                    assert 0 <= d1 <= 13
                    bidx[p, i] = d1
                elif ok[0]:
                    assert d1 == 2
                    bidx[p, i] = 14
                elif ok[1]:
                    assert d1 == 10
                    bidx[p, i] = 15
                else:
                    bidx[p, i] = 16
    return pos_tile, win, bidx.reshape(-1), flags


def _na_bias_tiles(rpb):
    cols = np.arange(GRID_W)
    cstart = np.clip(cols - NA_COLS // 2, 0, GRID_W - NA_COLS)
    kc = cols[:, None]
    c = cols[None, :]
    col_ok = (kc >= cstart[None, :]) & (kc < cstart[None, :] + NA_COLS)
    dc = np.clip(kc - c + NA_COLS - 1, 0, 2 * NA_COLS - 2)
    onehot = ((np.arange(2 * NA_COLS - 1)[:, None, None] == dc[None]) & col_ok[None]).astype(np.float32)
    tc = jnp.einsum('lhdk,kn->lhdn', rpb.astype(jnp.float32) * LOG2E, onehot.reshape(2 * NA_COLS - 1, -1),
                    precision=lax.Precision.HIGHEST)
    tc = tc.reshape(tc.shape[:3] + (GRID_W, GRID_W)) + np.where(col_ok, 0.0, NEG).astype(np.float32)
    neg = jnp.full(tc.shape[:2] + (1, GRID_W, GRID_W), NEG, jnp.float32)
    both = jnp.concatenate([tc[:, :, 1:15], tc[:, :, 0:14]], axis=-1)
    first_only = jnp.concatenate([tc[:, :, 3:4], neg], axis=-1)
    second_only = jnp.concatenate([neg, tc[:, :, 10:11]], axis=-1)
    none = jnp.concatenate([neg, neg], axis=-1)
    tiles = jnp.concatenate([both, first_only, second_only, none], axis=2)
    l = tiles.shape[0]
    return tiles.reshape(l, NA_HEADS * N_BIAS_TILES, GRID_W, PAIR)


def _rope_tables(max_len):
    half = HEAD_DIM // 2
    inv = ROPE_THETA ** (-jnp.arange(half, dtype=jnp.float32) / half)
    ang = jnp.arange(max_len, dtype=jnp.float32)[:, None] * inv[None, :]
    cos = jnp.tile(jnp.cos(ang), (1, LANES // half))
    sin = jnp.tile(jnp.sin(ang), (1, LANES // half))
    sign = np.where((np.arange(LANES) % HEAD_DIM) < half, -1.0, 1.0).astype(np.float32)
    return cos, sin * sign[None, :]


def kernel(x_prompt, x_sample, norm_mix, w_in, rpb, sink, w_out, norm_mlp, w_up, w_down, norm_final):
    depth = w_in.shape[0]
    seq_lens = [x_prompt.shape[1]] * x_prompt.shape[0] + [x_sample.shape[1]] * x_sample.shape[0]
    pos_tile, win, bidx, flags = (jnp.asarray(a) for a in _static_tables(seq_lens))
    cos, sin = _rope_tables(max(seq_lens))

    bf = jnp.bfloat16
    scale = HEAD_DIM ** -0.5 * LOG2E
    o1, o2, o3 = NA_WIDTH, 2 * NA_WIDTH, 3 * NA_WIDTH
    o4 = o3 + SWA_WIDTH
    o5 = o4 + SWA_KV_WIDTH
    wtok = jnp.concatenate([w_in[..., :o1] * scale, w_in[..., o1:o2], w_in[..., o3:o4] * scale,
                            w_in[..., o4:o5]], axis=-1).astype(bf)
    wfm = jnp.swapaxes(jnp.concatenate([w_in[..., o2:o3], w_in[..., o5:]], axis=-1).astype(bf), 1, 2)
    wout = w_out.astype(bf)
    wup = w_up.astype(bf)
    wdn = w_down.astype(bf)
    bias = _na_bias_tiles(rpb)
    heads = np.array([[4 * kv + vi, 4 * kv + 2 + vi] for kv in range(SWA_KV_HEADS) for vi in range(2)])
    sinkvec = jnp.repeat(sink.astype(jnp.float32)[:, heads] * LOG2E, PAIR, axis=-1).reshape(depth, 4, 2 * PAIR)
    sinkvec = jnp.concatenate([sinkvec, jnp.zeros_like(sinkvec)], axis=1)

    xs = (x_prompt.reshape(-1, D_MODEL), x_sample.reshape(-1, D_MODEL))
    n_first = xs[0].shape[0] // TOK_TILE
    gf = norm_final.reshape(1, D_MODEL)
    for l in range(depth):
        qa, ka, qs, ksa, ksb, vat, vst = _inproj(xs, norm_mix[l].reshape(1, D_MODEL), wtok[l], wfm[l],
                                                 cos, sin, pos_tile)
        oa, ob = _attn(qa, ka, vat, bias[l], win, bidx, qs, ksa, ksb, vst, sinkvec[l], flags)
        out = _mlp(xs, oa, ob, wout[l], norm_mlp[l].reshape(1, D_MODEL), wup[l], wdn[l], gf,
                   final=(l == depth - 1), n_first=n_first)
        xs = (out,)
    y_prompt, y_sample = out
    return (y_prompt.reshape(x_prompt.shape), y_sample.reshape(x_sample.shape))
```

```python
import functools

import numpy as np
import jax
import jax.numpy as jnp
from jax import lax
from jax.experimental import pallas as pl
from jax.experimental.pallas import tpu as pltpu

D_MODEL = 1024
HEAD_DIM = 64
NA_HEADS = 8
SWA_HEADS = 8
SWA_KV_HEADS = 2
NA_WIDTH = NA_HEADS * HEAD_DIM
SWA_WIDTH = SWA_HEADS * HEAD_DIM
SWA_KV_WIDTH = SWA_KV_HEADS * HEAD_DIM
D_FF = 4 * D_MODEL
GRID_W = 64
NA_ROWS = 8
NA_COLS = 16
WINDOW = 128
ROPE_THETA = 10000.0
EPS = 1e-5
NEG = -1e30
LOG2E = 1.4426950408889634

LANES = 128
TOK_TILE = 512
PAIR = 2 * GRID_W
PAIRS_PER_TILE = TOK_TILE // PAIR
ATT_TILE = 4096
ATT_PAIRS = ATT_TILE // PAIR
NA_WIN_ROWS = 9
NA_WIN_PAIRS = (NA_WIN_ROWS + 1) // 2
NA_EDGE_PAIRS = 2
NA_HALO_PAIRS = ATT_PAIRS + 2 * NA_EDGE_PAIRS
SUBLANES_BF16 = 16
N_BIAS_TILES = 17
VMEM_LIMIT = 56 * 1024 * 1024

TOK_COLS = NA_WIDTH + NA_WIDTH + SWA_WIDTH + SWA_KV_WIDTH
FM_ROWS = NA_WIDTH + SWA_KV_WIDTH


def _rms(x, g):
    return x * lax.rsqrt(jnp.mean(x * x, axis=-1, keepdims=True) + EPS) * g


def _dot_nt(a, b):
    return lax.dot_general(a, b, (((1,), (1,)), ((), ())), preferred_element_type=jnp.float32)


def _dot(a, b):
    return jnp.dot(a, b, preferred_element_type=jnp.float32)


def _token_specs(n_first, width, n_sources):
    if n_sources == 1:
        return [pl.BlockSpec((TOK_TILE, width), lambda i, *_: (i, 0))]
    return [pl.BlockSpec((TOK_TILE, width), lambda i, *_: (jnp.minimum(i, n_first - 1), 0)),
            pl.BlockSpec((TOK_TILE, width), lambda i, *_: (jnp.maximum(i - n_first, 0), 0))]


def _load_tokens(refs, n_first):
    if len(refs) == 1:
        return refs[0][...]
    return jnp.where(pl.program_id(0) < n_first, refs[0][...], refs[1][...])


def _rope(x, cos, sin_signed, first_half):
    rot = jnp.where(first_half, pltpu.roll(x, LANES - HEAD_DIM // 2, 1), pltpu.roll(x, HEAD_DIM // 2, 1))
    return x * cos + rot * sin_signed


def _inproj_kernel(pos_ref, *refs, n_first, n_sources):
    del pos_ref
    x_refs, refs = refs[:n_sources], refs[n_sources:]
    (g_ref, wtok_ref, wfm_ref, cos_ref, sin_ref,
     qa_ref, ka_ref, qs_ref, ksa_ref, ksb_ref, vat_ref, vst_ref) = refs
    h = _rms(_load_tokens(x_refs, n_first), g_ref[...]).astype(jnp.bfloat16)
    proj = _dot(h, wtok_ref[...])
    qa_ref[...] = proj[:, :NA_WIDTH].astype(jnp.bfloat16)
    ka_ref[...] = proj[:, NA_WIDTH:2 * NA_WIDTH].astype(jnp.bfloat16)
    cos = cos_ref[...]
    sin = sin_ref[...]
    lane = lax.broadcasted_iota(jnp.int32, cos.shape, 1)
    first_half = (lane % HEAD_DIM) < (HEAD_DIM // 2)
    base = 2 * NA_WIDTH
    for j in range(SWA_WIDTH // LANES):
        q = proj[:, base + j * LANES: base + (j + 1) * LANES]
        qs_ref[:, j * LANES:(j + 1) * LANES] = _rope(q, cos, sin, first_half).astype(jnp.bfloat16)
    k = _rope(proj[:, base + SWA_WIDTH:], cos, sin, first_half)
    ksa_ref[...] = k.astype(jnp.bfloat16)
    ksb_ref[...] = pltpu.roll(k, HEAD_DIM, 1).astype(jnp.bfloat16)
    fm = _dot_nt(wfm_ref[...], h)
    for j in range(TOK_TILE // PAIR):
        vat_ref[j] = fm[:NA_WIDTH, j * PAIR:(j + 1) * PAIR].astype(jnp.bfloat16)
        vst_ref[j] = fm[NA_WIDTH:, j * PAIR:(j + 1) * PAIR].astype(jnp.bfloat16)


def _inproj(xs, g, wtok, wfm, cos, sin, pos_tile):
    t = sum(x.shape[0] for x in xs)
    n = t // TOK_TILE
    n_first = xs[0].shape[0] // TOK_TILE
    npb = t // PAIR
    bf = jnp.bfloat16
    tile = lambda w: pl.BlockSpec((TOK_TILE, w), lambda i, p: (i, 0))
    whole = lambda a: pl.BlockSpec(a.shape, lambda i, p: (0,) * a.ndim)
    rope_spec = pl.BlockSpec((TOK_TILE, LANES), lambda i, p: (p[i], 0))
    fm_spec = lambda r: pl.BlockSpec((PAIRS_PER_TILE, r, PAIR), lambda i, p: (i, 0, 0))
    return pl.pallas_call(
        functools.partial(_inproj_kernel, n_first=n_first, n_sources=len(xs)),
        grid_spec=pltpu.PrefetchScalarGridSpec(
            num_scalar_prefetch=1, grid=(n,),
            in_specs=_token_specs(n_first, D_MODEL, len(xs))
                     + [whole(g), whole(wtok), whole(wfm), rope_spec, rope_spec],
            out_specs=[tile(NA_WIDTH), tile(NA_WIDTH), tile(SWA_WIDTH), tile(SWA_KV_WIDTH),
                       tile(SWA_KV_WIDTH), fm_spec(NA_WIDTH), fm_spec(SWA_KV_WIDTH)]),
        out_shape=[jax.ShapeDtypeStruct((t, NA_WIDTH), bf), jax.ShapeDtypeStruct((t, NA_WIDTH), bf),
                   jax.ShapeDtypeStruct((t, SWA_WIDTH), bf), jax.ShapeDtypeStruct((t, SWA_KV_WIDTH), bf),
                   jax.ShapeDtypeStruct((t, SWA_KV_WIDTH), bf),
                   jax.ShapeDtypeStruct((npb, NA_WIDTH, PAIR), bf),
                   jax.ShapeDtypeStruct((npb, SWA_KV_WIDTH, PAIR), bf)],
        compiler_params=pltpu.CompilerParams(dimension_semantics=("arbitrary",),
                                             vmem_limit_bytes=VMEM_LIMIT),
        name="inproj",
    )(pos_tile, *xs, g, wtok, wfm, cos, sin)


def _pipeline(items, scores, softmax, values):
    s_val, p_val = {}, {}
    n = len(items)

    def step(t):
        if t < n:
            s_val[t] = scores(*items[t])
        if 0 <= t - 1 < n:
            p_val[t - 1] = softmax(*items[t - 1], s_val.pop(t - 1))
        if 0 <= t - 2 < n:
            values(*items[t - 2], p_val.pop(t - 2))

    return [functools.partial(step, t) for t in range(n + 2)]


def _na_steps(step, win_ref, bidx_ref, q_ref, kp_ref, kc_ref, kn_ref, vp_ref, vc_ref, vn_ref, bias_ref,
              o_ref, kbuf, vbuf):
    edge = NA_EDGE_PAIRS * PAIR
    kbuf[:edge, :] = kp_ref[...]
    kbuf[edge:edge + ATT_TILE, :] = kc_ref[...]
    kbuf[edge + ATT_TILE:, :] = kn_ref[...]
    vbuf[:NA_EDGE_PAIRS] = vp_ref[...]
    vbuf[NA_EDGE_PAIRS:NA_EDGE_PAIRS + ATT_PAIRS] = vc_ref[...]
    vbuf[NA_EDGE_PAIRS + ATT_PAIRS:] = vn_ref[...]
    lane = lax.broadcasted_iota(jnp.int32, (PAIR, LANES), 1)
    low = lane < HEAD_DIM
    zero = jnp.zeros((PAIR, LANES), jnp.bfloat16)

    rels = [win_ref[step * ATT_PAIRS + pp] for pp in range(ATT_PAIRS)]

    def scores(pp, hp):
        cols = slice(hp * LANES, (hp + 1) * LANES)
        qp = q_ref[pp * PAIR:(pp + 1) * PAIR, cols]
        qbd = jnp.concatenate([jnp.where(low, qp, zero), jnp.where(low, zero, qp)], axis=0)
        krow = pl.multiple_of(rels[pp] * PAIR, PAIR)
        kp = kbuf[pl.ds(krow, NA_WIN_ROWS * GRID_W), cols]
        s = _dot_nt(kp, qbd)
        p = step * ATT_PAIRS + pp
        pieces = []
        for i in range(NA_WIN_ROWS):
            bi = bidx_ref[p * NA_WIN_ROWS + i]
            b = jnp.concatenate([bias_ref[(2 * hp) * N_BIAS_TILES + bi],
                                 bias_ref[(2 * hp + 1) * N_BIAS_TILES + bi]], axis=1)
            pieces.append(s[i * GRID_W:(i + 1) * GRID_W, :] + b)
        s = jnp.concatenate(pieces, axis=0)
        return s, jnp.max(s, axis=0, keepdims=True)

    def softmax(pp, hp, sm):
        s, m = sm
        e = jnp.exp2(s - m)
        pad = jnp.zeros((NA_WIN_PAIRS * PAIR - NA_WIN_ROWS * GRID_W, 2 * LANES), jnp.bfloat16)
        return jnp.concatenate([e.astype(jnp.bfloat16), pad], axis=0), jnp.sum(e, axis=0, keepdims=True)

    def values(pp, hp, ed):
        e, denom = ed
        cols = slice(hp * LANES, (hp + 1) * LANES)
        vt = jnp.concatenate([vbuf[rels[pp] + j, cols, :] for j in range(NA_WIN_PAIRS)], axis=1)
        ot = _dot(vt, e) / denom
        o_pair = jnp.concatenate([ot[:HEAD_DIM, :LANES], ot[HEAD_DIM:, LANES:]], axis=0)
        o_ref[pp * PAIR:(pp + 1) * PAIR, cols] = o_pair.T.astype(jnp.bfloat16)

    items = [(pp, hp) for pp in range(ATT_PAIRS) for hp in range(NA_HEADS // 2)]
    return _pipeline(items, scores, softmax, values)


def _swa_steps(step, flag_ref, q_ref, kap_ref, kac_ref, kan_ref, kbp_ref, kbc_ref, kbn_ref,
               vp_ref, vc_ref, vn_ref, sink_ref, o_ref, kabuf, kbbuf, vbuf):
    nblk = ATT_PAIRS
    kabuf[:PAIR] = kap_ref[...]
    kabuf[PAIR:PAIR + ATT_TILE] = kac_ref[...]
    kabuf[PAIR + ATT_TILE:] = kan_ref[...]
    kbbuf[:PAIR] = kbp_ref[...]
    kbbuf[PAIR:PAIR + ATT_TILE] = kbc_ref[...]
    kbbuf[PAIR + ATT_TILE:] = kbn_ref[...]
    vbuf[0:1] = vp_ref[...]
    vbuf[1:1 + nblk] = vc_ref[...]
    vbuf[1 + nblk:] = vn_ref[...]

    nkeys = 3 * PAIR
    ki = lax.broadcasted_iota(jnp.int32, (nkeys, LANES), 0)
    qi = lax.broadcasted_iota(jnp.int32, (nkeys, LANES), 1)
    rel = ki - qi
    band = jnp.where(rel >= 0, jnp.where(rel <= 2 * WINDOW, 0.0, NEG), NEG).astype(jnp.float32)
    in_prev = ki < PAIR
    in_next = ki >= 2 * PAIR
    low = qi < HEAD_DIM
    zero = jnp.zeros((nkeys, LANES), jnp.bfloat16)
    ones = jnp.ones((SUBLANES_BF16, nkeys), jnp.bfloat16)

    masks = []
    for qb in range(nblk):
        flags = flag_ref[step * nblk + qb]
        pen_prev = (flags & 1).astype(jnp.float32) * NEG
        pen_next = ((flags >> 1) & 1).astype(jnp.float32) * NEG
        mask = band + jnp.where(in_prev, pen_prev, 0.0) + jnp.where(in_next, pen_next, 0.0)
        masks.append((jnp.concatenate([mask[:PAIR], mask[:PAIR]], axis=1),
                      jnp.concatenate([mask[2 * PAIR:], mask[2 * PAIR:]], axis=1)))

    def scores(qb, kv):
        rows = slice(qb * PAIR, (qb + 1) * PAIR)
        q_even = q_ref[rows, (2 * kv) * LANES:(2 * kv + 1) * LANES]
        q_odd = q_ref[rows, (2 * kv + 1) * LANES:(2 * kv + 2) * LANES]
        qcat = jnp.concatenate([q_even, q_odd], axis=0)
        ka = kabuf[qb * PAIR:qb * PAIR + nkeys, :]
        kb = kbbuf[qb * PAIR:qb * PAIR + nkeys, :]
        k_lo = jnp.where(low, ka if kv == 0 else kb, zero)
        k_hi = jnp.where(low, zero, kb if kv == 0 else ka)
        return _dot_nt(k_lo, qcat), _dot_nt(k_hi, qcat)

    def softmax(qb, kv, s_pair):
        probs, sink_terms = [], []
        for vi, s in enumerate(s_pair):
            s = jnp.concatenate([s[:PAIR] + masks[qb][0], s[PAIR:2 * PAIR], s[2 * PAIR:] + masks[qb][1]], axis=0)
            snk = sink_ref[2 * kv + vi:2 * kv + vi + 1, :]
            m = jnp.maximum(jnp.max(s, axis=0, keepdims=True), snk)
            probs.append(jnp.exp2(s - m).astype(jnp.bfloat16))
            sink_terms.append(jnp.exp2(snk - m))
        return jnp.concatenate(probs, axis=1), jnp.concatenate(sink_terms, axis=1)

    def values(qb, kv, ps):
        pt, sink_term = ps
        vt = jnp.concatenate([vbuf[qb + j] for j in range(3)], axis=1)
        ot = _dot(jnp.concatenate([vt[kv * HEAD_DIM:(kv + 1) * HEAD_DIM, :], ones], axis=0), pt)
        ot = ot[:HEAD_DIM] / (ot[HEAD_DIM:HEAD_DIM + 1] + sink_term)
        for pair in range(2):
            o_pair = jnp.concatenate([ot[:, pair * LANES:(pair + 1) * LANES],
                                      ot[:, (2 + pair) * LANES:(3 + pair) * LANES]], axis=0)
            o_ref[qb * PAIR:(qb + 1) * PAIR, (2 * kv + pair) * LANES:(2 * kv + pair + 1) * LANES] = (
                o_pair.T.astype(jnp.bfloat16))

    items = [(qb, kv) for qb in range(nblk) for kv in range(SWA_KV_HEADS)]
    return _pipeline(items, scores, softmax, values)


def _na_kernel(win_ref, bidx_ref, *refs):
    for run_step in _na_steps(pl.program_id(0), win_ref, bidx_ref, *refs):
        run_step()


def _swa_kernel(flag_ref, *refs):
    for run_step in _swa_steps(pl.program_id(0), flag_ref, *refs):
        run_step()


def _attn(qa, ka, vat, bias, win, bidx, qs, ksa, ksb, vst, sinkvec, flags):
    t = qa.shape[0]
    n = t // ATT_TILE
    npb = t // PAIR
    edge = NA_EDGE_PAIRS * PAIR
    per = ATT_TILE // edge
    na_prev = lambda i, *_: jnp.maximum(per * i - 1, 0)
    na_next = lambda i, *_: jnp.minimum(per * i + per, t // edge - 1)
    sw_prev = lambda i, *_: jnp.maximum(ATT_PAIRS * i - 1, 0)
    sw_next = lambda i, *_: jnp.minimum(ATT_PAIRS * i + ATT_PAIRS, npb - 1)
    cur = lambda i, *_: i

    def rows(block, width, idx):
        return pl.BlockSpec((block, width), lambda *a: (idx(*a), 0))

    def slabs(block, width, idx):
        return pl.BlockSpec((block, width, PAIR), lambda *a: (idx(*a), 0, 0))

    whole = lambda a: pl.BlockSpec(a.shape, lambda *_: (0,) * a.ndim)
    na_specs = ([rows(ATT_TILE, NA_WIDTH, cur)]
                + [rows(edge, NA_WIDTH, na_prev), rows(ATT_TILE, NA_WIDTH, cur), rows(edge, NA_WIDTH, na_next)]
                + [slabs(NA_EDGE_PAIRS, NA_WIDTH, na_prev), slabs(ATT_PAIRS, NA_WIDTH, cur),
                   slabs(NA_EDGE_PAIRS, NA_WIDTH, na_next)]
                + [whole(bias)])
    sw_k = [rows(PAIR, SWA_KV_WIDTH, sw_prev), rows(ATT_TILE, SWA_KV_WIDTH, cur), rows(PAIR, SWA_KV_WIDTH, sw_next)]
    swa_specs = ([rows(ATT_TILE, SWA_WIDTH, cur)] + sw_k + sw_k
                 + [slabs(1, SWA_KV_WIDTH, sw_prev), slabs(ATT_PAIRS, SWA_KV_WIDTH, cur),
                    slabs(1, SWA_KV_WIDTH, sw_next)]
                 + [whole(sinkvec)])
    bf = jnp.bfloat16
    params = pltpu.CompilerParams(dimension_semantics=("arbitrary",), vmem_limit_bytes=VMEM_LIMIT)
    oa = pl.pallas_call(
        _na_kernel,
        grid_spec=pltpu.PrefetchScalarGridSpec(
            num_scalar_prefetch=2, grid=(n,),
            in_specs=na_specs,
            out_specs=rows(ATT_TILE, NA_WIDTH, cur),
            scratch_shapes=[pltpu.VMEM((NA_HALO_PAIRS * PAIR, NA_WIDTH), bf),
                            pltpu.VMEM((NA_HALO_PAIRS, NA_WIDTH, PAIR), bf)]),
        out_shape=jax.ShapeDtypeStruct((t, NA_WIDTH), bf),
        compiler_params=params,
        name="na_attn",
    )(win, bidx, qa, ka, ka, ka, vat, vat, vat, bias)
    ob = pl.pallas_call(
        _swa_kernel,
        grid_spec=pltpu.PrefetchScalarGridSpec(
            num_scalar_prefetch=1, grid=(n,),
            in_specs=swa_specs,
            out_specs=rows(ATT_TILE, SWA_WIDTH, cur),
            scratch_shapes=[pltpu.VMEM((ATT_TILE + 2 * PAIR, SWA_KV_WIDTH), bf),
                            pltpu.VMEM((ATT_TILE + 2 * PAIR, SWA_KV_WIDTH), bf),
                            pltpu.VMEM((ATT_PAIRS + 2, SWA_KV_WIDTH, PAIR), bf)]),
        out_shape=jax.ShapeDtypeStruct((t, SWA_WIDTH), bf),
        compiler_params=params,
        name="swa_attn",
    )(flags, qs, ksa, ksa, ksa, ksb, ksb, ksb, vst, vst, vst, sinkvec)
    return oa, ob


def _mlp_kernel(*refs, final, n_first, n_sources):
    x_refs, refs = refs[:n_sources], refs[n_sources:]
    oa_ref, ob_ref, wout_ref, g_ref, wup_ref, wdn_ref, gf_ref = refs[:7]
    out_refs = refs[7:]
    mix = jnp.concatenate([oa_ref[...], ob_ref[...]], axis=1)
    x1 = _load_tokens(x_refs, n_first) + _dot(mix, wout_ref[...])
    h = _rms(x1, g_ref[...]).astype(jnp.bfloat16)
    acc = x1
    for c in range(D_FF // D_MODEL):
        cols = slice(c * D_MODEL, (c + 1) * D_MODEL)
        u = jnp.maximum(_dot(h, wup_ref[:, cols]), 0.0)
        acc = acc + _dot((u * u).astype(jnp.bfloat16), wdn_ref[cols, :])
    if not final:
        out_refs[0][...] = acc
        return
    y = _rms(acc, gf_ref[...])
    step = pl.program_id(0)

    @pl.when(step < n_first)
    def _():
        out_refs[0][...] = y

    @pl.when(step >= n_first)
    def _():
        out_refs[1][...] = y


def _mlp(xs, oa, ob, wout, g, wup, wdn, gf, final, n_first):
    t = oa.shape[0]
    n = t // TOK_TILE
    tile = lambda w: pl.BlockSpec((TOK_TILE, w), lambda i: (i, 0))
    whole = lambda a: pl.BlockSpec(a.shape, lambda i: (0,) * a.ndim, pipeline_mode=pl.Buffered(1))
    if final:
        out_specs = _token_specs(n_first, D_MODEL, 2)
        out_shape = [jax.ShapeDtypeStruct((n_first * TOK_TILE, D_MODEL), jnp.float32),
                     jax.ShapeDtypeStruct((t - n_first * TOK_TILE, D_MODEL), jnp.float32)]
    else:
        out_specs = tile(D_MODEL)
        out_shape = jax.ShapeDtypeStruct((t, D_MODEL), jnp.float32)
    return pl.pallas_call(
        functools.partial(_mlp_kernel, final=final, n_first=n_first, n_sources=len(xs)),
        grid=(n,),
        in_specs=_token_specs(n_first, D_MODEL, len(xs))
                 + [tile(NA_WIDTH), tile(SWA_WIDTH), whole(wout), whole(g), whole(wup), whole(wdn), whole(gf)],
        out_specs=out_specs,
        out_shape=out_shape,
        compiler_params=pltpu.CompilerParams(dimension_semantics=("arbitrary",),
                                             vmem_limit_bytes=VMEM_LIMIT),
        name="mlp",
    )(*xs, oa, ob, wout, g, wup, wdn, gf)


def _static_tables(seq_lens):
    starts = np.concatenate([[0], np.cumsum(seq_lens)[:-1]]).astype(np.int64)
    total = int(np.sum(seq_lens))
    pos_tile = np.zeros(total // TOK_TILE, np.int32)
    win = np.zeros(total // PAIR, np.int32)
    bidx = np.zeros((total // PAIR, NA_WIN_ROWS), np.int32)
    flags = np.zeros(total // PAIR, np.int32)
    for s0, s in zip(starts, seq_lens):
        assert s % ATT_TILE == 0 and s0 % ATT_TILE == 0
        rows = s // GRID_W
        for i in range(s // TOK_TILE):
            pos_tile[s0 // TOK_TILE + i] = i
        flags[s0 // PAIR] |= 1
        flags[(s0 + s) // PAIR - 1] |= 2
        for q in range(rows // 2):
            p = s0 // PAIR + q
            r0 = 2 * q
            kb = min(max(r0 - NA_ROWS // 2, 0), rows - NA_ROWS)
            assert kb % 2 == 0
            rel = s0 // PAIR + kb // 2 - ((p // ATT_PAIRS) * ATT_PAIRS - NA_EDGE_PAIRS)
            assert 0 <= rel and rel + NA_WIN_PAIRS <= NA_HALO_PAIRS
            win[p] = rel
            for i in range(NA_WIN_ROWS):
                kr = kb + i
                ok = []
                for r in (r0, r0 + 1):
                    rs = min(max(r - NA_ROWS // 2, 0), rows - NA_ROWS)
                    ok.append(rs <= kr < rs + NA_ROWS)
                d1 = kr - (r0 + 1) + NA_ROWS - 1
                if ok[0] and ok[1]:
                    assert 0 <= d1 <= 13
                    bidx[p, i] = d1
                elif ok[0]:
                    assert d1 == 2
                    bidx[p, i] = 14
                elif ok[1]:
                    assert d1 == 10
                    bidx[p, i] = 15
                else:
                    bidx[p, i] = 16
    return pos_tile, win, bidx.reshape(-1), flags


def _na_bias_tiles(rpb):
    cols = np.arange(GRID_W)
    cstart = np.clip(cols - NA_COLS // 2, 0, GRID_W - NA_COLS)
    kc = cols[:, None]
    c = cols[None, :]
    col_ok = (kc >= cstart[None, :]) & (kc < cstart[None, :] + NA_COLS)
    dc = np.clip(kc - c + NA_COLS - 1, 0, 2 * NA_COLS - 2)
    onehot = ((np.arange(2 * NA_COLS - 1)[:, None, None] == dc[None]) & col_ok[None]).astype(np.float32)
    tc = jnp.einsum('lhdk,kn->lhdn', rpb.astype(jnp.float32) * LOG2E, onehot.reshape(2 * NA_COLS - 1, -1),
                    precision=lax.Precision.HIGHEST)
    tc = tc.reshape(tc.shape[:3] + (GRID_W, GRID_W)) + np.where(col_ok, 0.0, NEG).astype(np.float32)
    neg = jnp.full(tc.shape[:2] + (1, GRID_W, GRID_W), NEG, jnp.float32)
    both = jnp.concatenate([tc[:, :, 1:15], tc[:, :, 0:14]], axis=-1)
    first_only = jnp.concatenate([tc[:, :, 3:4], neg], axis=-1)
    second_only = jnp.concatenate([neg, tc[:, :, 10:11]], axis=-1)
    none = jnp.concatenate([neg, neg], axis=-1)
    tiles = jnp.concatenate([both, first_only, second_only, none], axis=2)
    l = tiles.shape[0]
    return tiles.reshape(l, NA_HEADS * N_BIAS_TILES, GRID_W, PAIR)


def _rope_tables(max_len):
    half = HEAD_DIM // 2
    inv = ROPE_THETA ** (-jnp.arange(half, dtype=jnp.float32) / half)
    ang = jnp.arange(max_len, dtype=jnp.float32)[:, None] * inv[None, :]
    cos = jnp.tile(jnp.cos(ang), (1, LANES // half))
    sin = jnp.tile(jnp.sin(ang), (1, LANES // half))
    sign = np.where((np.arange(LANES) % HEAD_DIM) < half, -1.0, 1.0).astype(np.float32)
    return cos, sin * sign[None, :]


def kernel(x_prompt, x_sample, norm_mix, w_in, rpb, sink, w_out, norm_mlp, w_up, w_down, norm_final):
    depth = w_in.shape[0]
    seq_lens = [x_prompt.shape[1]] * x_prompt.shape[0] + [x_sample.shape[1]] * x_sample.shape[0]
    pos_tile, win, bidx, flags = (jnp.asarray(a) for a in _static_tables(seq_lens))
    cos, sin = _rope_tables(max(seq_lens))

    bf = jnp.bfloat16
    scale = HEAD_DIM ** -0.5 * LOG2E
    o1, o2, o3 = NA_WIDTH, 2 * NA_WIDTH, 3 * NA_WIDTH
    o4 = o3 + SWA_WIDTH
    o5 = o4 + SWA_KV_WIDTH
    wtok = jnp.concatenate([w_in[..., :o1] * scale, w_in[..., o1:o2], w_in[..., o3:o4] * scale,
                            w_in[..., o4:o5]], axis=-1).astype(bf)
    wfm = jnp.swapaxes(jnp.concatenate([w_in[..., o2:o3], w_in[..., o5:]], axis=-1).astype(bf), 1, 2)
    wout = w_out.astype(bf)
    wup = w_up.astype(bf)
    wdn = w_down.astype(bf)
    bias = _na_bias_tiles(rpb)
    heads = np.array([[4 * kv + vi, 4 * kv + 2 + vi] for kv in range(SWA_KV_HEADS) for vi in range(2)])
    sinkvec = jnp.repeat(sink.astype(jnp.float32)[:, heads] * LOG2E, PAIR, axis=-1).reshape(depth, 4, 2 * PAIR)
    sinkvec = jnp.concatenate([sinkvec, jnp.zeros_like(sinkvec)], axis=1)

    xs = (x_prompt.reshape(-1, D_MODEL), x_sample.reshape(-1, D_MODEL))
    n_first = xs[0].shape[0] // TOK_TILE
    gf = norm_final.reshape(1, D_MODEL)
    for l in range(depth):
        qa, ka, qs, ksa, ksb, vat, vst = _inproj(xs, norm_mix[l].reshape(1, D_MODEL), wtok[l], wfm[l],
                                                 cos, sin, pos_tile)
        oa, ob = _attn(qa, ka, vat, bias[l], win, bidx, qs, ksa, ksb, vst, sinkvec[l], flags)
        out = _mlp(xs, oa, ob, wout[l], norm_mlp[l].reshape(1, D_MODEL), wup[l], wdn[l], gf,
                   final=(l == depth - 1), n_first=n_first)
        xs = (out,)
    y_prompt, y_sample = out
    return (y_prompt.reshape(x_prompt.shape), y_sample.reshape(x_sample.shape))
```
